```python
import math
import jax
import jax.numpy as jnp
from jax import lax

D_MODEL = 4096
BATCH = 4
SEQ = 4096
DEPTH = 1

CTX_LEN = 256
GRID_W = 64
EPS = 1e-6

POOL_W = D_MODEL // 2
POOL_GROUPS = 4
POOL_GC = POOL_W // POOL_GROUPS
POOL_WINDOWS = (2, 4, 8, 16)

SSD_INNER = D_MODEL // 2
SSD_HEAD_DIM = 64
SSD_HEADS = SSD_INNER // SSD_HEAD_DIM
SSD_GROUPS = 8
SSD_REP = SSD_HEADS // SSD_GROUPS
SSD_STATE = 128
SSD_CONV = 5
SSD_CHUNK = 128
SSD_GN = SSD_GROUPS * SSD_STATE
XBC_W = SSD_INNER + 2 * SSD_GN

OFF_Z = POOL_W
OFF_XBC = OFF_Z + SSD_INNER
OFF_DT = OFF_XBC + XBC_W
IN_W = OFF_DT + 2 * SSD_HEADS
MIX_W = POOL_W + SSD_INNER

PEER_HEADS = 8
PEER_NKEYS = 128
PEER_EXPERTS = PEER_NKEYS * PEER_NKEYS
PEER_KEY_DIM = 256
PEER_HALF = PEER_KEY_DIM // 2
PEER_TOPK = 16
PEER_BLOCK = 128

kernel_name = 'hybrid_pool_ssd_peer_prefix_dit'


def rmsnorm(x, g):
    xf = x.astype(jnp.float32)
    y = xf * lax.rsqrt(jnp.mean(xf * xf, axis=-1, keepdims=True) + EPS)
    return (y * g.astype(jnp.float32)).astype(x.dtype)


def window_mean(v, w, axis):
    n = v.shape[axis]
    vf = v.astype(jnp.float32)
    zero_shape = list(vf.shape)
    zero_shape[axis] = 1
    cs = jnp.concatenate([jnp.zeros(zero_shape, jnp.float32), jnp.cumsum(vf, axis=axis)], axis=axis)
    t = jnp.arange(n)
    lo = jnp.clip(t - w // 2, 0, n)
    hi = jnp.clip(t - w // 2 + w, 0, n)
    total = jnp.take(cs, hi, axis=axis) - jnp.take(cs, lo, axis=axis)
    cnt_shape = [1] * vf.ndim
    cnt_shape[axis] = n
    cnt = (hi - lo).astype(jnp.float32).reshape(cnt_shape)
    return (total / cnt).astype(v.dtype)


def pool_mixer(v, w_grp, scale, rows):
    bsz, n, _ = v.shape
    vg = v.reshape(bsz, n, POOL_GROUPS, POOL_GC)
    diffs = []
    for k, w in enumerate(POOL_WINDOWS):
        vk = vg[:, :, k]
        if rows is None:
            m = window_mean(vk, w, 1)
        else:
            grid = vk.reshape(bsz, rows, GRID_W, POOL_GC)
            m = window_mean(window_mean(grid, w, 1), w, 2).reshape(bsz, n, POOL_GC)
        diffs.append(m - vk)
    p = jnp.stack(diffs, axis=2)
    y = jnp.einsum('bngc,gcd->bngd', p, w_grp).reshape(bsz, n, POOL_W)
    return y * scale


def dwconv(u, w, b):
    pad = SSD_CONV // 2
    y = lax.conv_general_dilated(u, w[:, None, :], window_strides=(1,), padding=[(pad, pad)],
                                 dimension_numbers=('NWC', 'WIO', 'NWC'),
                                 feature_group_count=u.shape[-1])
    return y + b


def ssd_scan(xh, dt, a, bm, cm, h0, with_output):
    bsz, n = xh.shape[:2]
    nc = n // SSD_CHUNK
    q = SSD_CHUNK
    xh = xh.reshape(bsz, nc, q, SSD_GROUPS, SSD_REP, SSD_HEAD_DIM)
    dt = dt.reshape(bsz, nc, q, SSD_GROUPS, SSD_REP)
    bm = bm.reshape(bsz, nc, q, SSD_GROUPS, SSD_STATE)
    cm = cm.reshape(bsz, nc, q, SSD_GROUPS, SSD_STATE)
    a_cs = jnp.cumsum(dt * a.reshape(SSD_GROUPS, SSD_REP), axis=2)
    decay_end = jnp.exp(a_cs[:, :, -1:] - a_cs)
    states = jnp.einsum('bcjgn,bcjgr,bcjgrp->bcgrpn', bm, decay_end * dt, xh)
    chunk_decay = jnp.exp(a_cs[:, :, -1])

    def step(h, inp):
        dec, st = inp
        return h * dec[..., None, None] + st, h

    h_last, h_prev = lax.scan(step, h0, (jnp.moveaxis(chunk_decay, 1, 0), jnp.moveaxis(states, 1, 0)))
    if not with_output:
        return None, h_last
    h_prev = jnp.moveaxis(h_prev, 0, 1)
    causal = jnp.tril(jnp.ones((q, q), bool))[:, :, None, None]
    seg = a_cs[:, :, :, None] - a_cs[:, :, None, :]
    decay_ij = jnp.exp(jnp.where(causal, seg, -jnp.inf))
    cb = jnp.einsum('bcign,bcjgn->bcijg', cm, bm)
    m = cb[..., None] * decay_ij * dt[:, :, None]
    y_diag = jnp.einsum('bcijgr,bcjgrp->bcigrp', m, xh)
    y_off = jnp.einsum('bcign,bcigr,bcgrpn->bcigrp', cm, jnp.exp(a_cs), h_prev)
    return (y_diag + y_off).reshape(bsz, n, SSD_HEADS, SSD_HEAD_DIM), h_last


def ssd_mixer(xbc_raw, dt_raw, conv_w, conv_b, dt_bias, a_log, h0_fwd, h0_bwd, with_output):
    bsz, n, _ = xbc_raw.shape
    xbc = jax.nn.silu(dwconv(xbc_raw, conv_w, conv_b)).astype(jnp.float32)
    xh = xbc[..., :SSD_INNER].reshape(bsz, n, SSD_HEADS, SSD_HEAD_DIM)
    bm = xbc[..., SSD_INNER:SSD_INNER + SSD_GN].reshape(bsz, n, SSD_GROUPS, SSD_STATE)
    cm = xbc[..., SSD_INNER + SSD_GN:].reshape(bsz, n, SSD_GROUPS, SSD_STATE)
    dt = jax.nn.softplus(dt_raw.astype(jnp.float32).reshape(bsz, n, 2, SSD_HEADS) + dt_bias.astype(jnp.float32))
    a = -jnp.exp(a_log.astype(jnp.float32))
    flip = lambda t: jnp.flip(t, axis=1)
    y_f, h_f = ssd_scan(xh, dt[:, :, 0], a[0], bm, cm, h0_fwd, with_output)
    y_b, h_b = ssd_scan(flip(xh), flip(dt[:, :, 1]), a[1], flip(bm), flip(cm), h0_bwd, with_output)
    y = (y_f + flip(y_b)) if with_output else None
    return y, xh, h_f, h_b


def mixer_output(proj, y_ssd, xh, pool_w, pool_scale, d_skip, ssd_norm, w_out, rows):
    bsz, n, _ = proj.shape
    pooled = pool_mixer(proj[..., :POOL_W], pool_w, pool_scale, rows)
    y = (y_ssd + d_skip.astype(jnp.float32)[:, None] * xh).reshape(bsz, n, SSD_INNER)
    z = proj[..., OFF_Z:OFF_XBC].astype(jnp.float32)
    ssd = rmsnorm(y * jax.nn.silu(z), ssd_norm).astype(proj.dtype)
    return jnp.concatenate([pooled.astype(proj.dtype), ssd], axis=-1) @ w_out


def peer_ffn(h, wq, sub_keys, u_tab, v_tab):
    bsz, n, d = h.shape
    tokens = bsz * n
    qry = (h @ wq).reshape(bsz, n, PEER_HEADS, 2, PEER_HALF).astype(jnp.float32)
    s = jnp.einsum('bnhsk,hsek->bnhse', qry, sub_keys.astype(jnp.float32))
    s_top, i_top = lax.top_k(s, PEER_TOPK)
    cand = (s_top[..., 0, :, None] + s_top[..., 1, None, :]).reshape(bsz, n, PEER_HEADS, PEER_TOPK * PEER_TOPK)
    cand_idx = (i_top[..., 0, :, None] * PEER_NKEYS + i_top[..., 1, None, :]).reshape(bsz, n, PEER_HEADS, PEER_TOPK * PEER_TOPK)
    g_top, pos = lax.top_k(cand, PEER_TOPK)
    idx = jnp.take_along_axis(cand_idx, pos, axis=-1)
    gate = jax.nn.softmax(g_top, axis=-1).astype(h.dtype)
    nb = tokens // PEER_BLOCK
    hb = h.reshape(nb, PEER_BLOCK, d)
    ib = idx.reshape(nb, PEER_BLOCK, PEER_HEADS * PEER_TOPK)
    gb = gate.reshape(nb, PEER_BLOCK, PEER_HEADS * PEER_TOPK)

    def block(args):
        hx, ix, gx = args
        u = jnp.take(u_tab, ix, axis=0)
        act = jax.nn.gelu(jnp.einsum('td,tkd->tk', hx, u), approximate=False)
        v = jnp.take(v_tab, ix, axis=0)
        return jnp.einsum('tk,tkd->td', act * gx, v)

    return lax.map(block, (hb, ib, gb)).reshape(bsz, n, d)


def setup_inputs(seed: int = 0) -> dict:
    key = jax.random.key(seed)
    ks = jax.random.split(key, 26)
    f32 = jnp.float32
    nrm = lambda k, shape, sc: jax.random.normal(k, shape, f32) * sc
    dt0 = jnp.exp(jax.random.uniform(ks[13], (DEPTH, 2, SSD_HEADS), f32, math.log(1e-3), math.log(1e-1)))
    return {
        'x': nrm(ks[0], (BATCH, SEQ, D_MODEL), 1.0),
        'c': nrm(ks[1], (BATCH, D_MODEL), 1.0),
        'ctx': nrm(ks[2], (BATCH, CTX_LEN, D_MODEL), 1.0),
        'c_ctx': nrm(ks[3], (D_MODEL,), 1.0),
        'w_mod': nrm(ks[4], (DEPTH, D_MODEL, 6 * D_MODEL), 0.5 * D_MODEL ** -0.5),
        'b_mod': nrm(ks[5], (DEPTH, 6 * D_MODEL), 0.02),
        'norm1': 1.0 + nrm(ks[6], (DEPTH, D_MODEL), 0.05),
        'norm2': 1.0 + nrm(ks[7], (DEPTH, D_MODEL), 0.05),
        'w_in': nrm(ks[8], (DEPTH, D_MODEL, IN_W), D_MODEL ** -0.5),
        'pool_w': nrm(ks[9], (DEPTH, POOL_GROUPS, POOL_GC, POOL_GC), POOL_GC ** -0.5),
        'pool_scale': 1.0 + nrm(ks[10], (DEPTH, POOL_W), 0.1),
        'conv_w': nrm(ks[11], (DEPTH, SSD_CONV, XBC_W), SSD_CONV ** -0.5),
        'conv_b': nrm(ks[12], (DEPTH, XBC_W), 0.02),
        'dt_bias': dt0 + jnp.log(-jnp.expm1(-dt0)),
        'a_log': jnp.log(jax.random.uniform(ks[14], (DEPTH, 2, SSD_HEADS), f32, 1.0, 16.0)),
        'd_skip': 1.0 + nrm(ks[15], (DEPTH, SSD_HEADS), 0.1),
        'ssd_norm': 1.0 + nrm(ks[16], (DEPTH, SSD_INNER), 0.05),
        'w_out': nrm(ks[17], (DEPTH, MIX_W, D_MODEL), MIX_W ** -0.5),
        'peer_wq': nrm(ks[18], (DEPTH, D_MODEL, PEER_HEADS * PEER_KEY_DIM), D_MODEL ** -0.5),
        'peer_keys': nrm(ks[19], (DEPTH, PEER_HEADS, 2, PEER_NKEYS, PEER_HALF), PEER_HALF ** -0.5),
        'peer_u': nrm(ks[20], (DEPTH, PEER_EXPERTS, D_MODEL), D_MODEL ** -0.5),
        'peer_v': nrm(ks[21], (DEPTH, PEER_EXPERTS, D_MODEL), (PEER_HEADS * PEER_TOPK) ** -0.5),
        'final_norm': 1.0 + nrm(ks[22], (D_MODEL,), 0.05),
    }


def reference(x, c, ctx, c_ctx, w_mod, b_mod, norm1, norm2, w_in, pool_w, pool_scale,
              conv_w, conv_b, dt_bias, a_log, d_skip, ssd_norm, w_out,
              peer_wq, peer_keys, peer_u, peer_v, final_norm):
    bsz, n_lat, _ = x.shape
    rows = n_lat // GRID_W
    h0 = jnp.zeros((bsz, SSD_GROUPS, SSD_REP, SSD_HEAD_DIM, SSD_STATE), jnp.float32)
    for l in range(DEPTH):
        last = l == DEPTH - 1
        mod = jax.nn.silu(c) @ w_mod[l] + b_mod[l]
        sh1, sc1, g1, sh2, sc2, g2 = jnp.split(mod[:, None, :], 6, axis=-1)
        cmod = jax.nn.silu(c_ctx) @ w_mod[l] + b_mod[l]
        csh1, csc1, cg1, csh2, csc2, cg2 = jnp.split(cmod, 6)
        hc = rmsnorm(ctx, norm1[l]) * (1.0 + csc1) + csh1
        pc = hc @ (w_in[l][:, OFF_XBC:] if last else w_in[l])
        pc_ssd = pc if last else pc[..., OFF_XBC:]
        yc, xc, hf, hb = ssd_mixer(pc_ssd[..., :XBC_W], pc_ssd[..., XBC_W:], conv_w[l], conv_b[l],
                                   dt_bias[l], a_log[l], h0, h0, not last)
        h = rmsnorm(x, norm1[l]) * (1.0 + sc1) + sh1
        p = h @ w_in[l]
        y, xs, _, _ = ssd_mixer(p[..., OFF_XBC:OFF_DT], p[..., OFF_DT:], conv_w[l], conv_b[l],
                                dt_bias[l], a_log[l], hf, hb, True)
        x = x + g1 * mixer_output(p, y, xs, pool_w[l], pool_scale[l], d_skip[l], ssd_norm[l], w_out[l], rows)
        x = x + g2 * peer_ffn(rmsnorm(x, norm2[l]) * (1.0 + sc2) + sh2,
                              peer_wq[l], peer_keys[l], peer_u[l], peer_v[l])
        if not last:
            ctx = ctx + cg1 * mixer_output(pc, yc, xc, pool_w[l], pool_scale[l], d_skip[l], ssd_norm[l], w_out[l], None)
            ctx = ctx + cg2 * peer_ffn(rmsnorm(ctx, norm2[l]) * (1.0 + csc2) + csh2,
                                       peer_wq[l], peer_keys[l], peer_u[l], peer_v[l])
    return rmsnorm(x, final_norm)
```

```python
import functools

import jax
import jax.numpy as jnp
from jax import lax
from jax.experimental import pallas as pl
from jax.experimental.pallas import tpu as pltpu

F32 = jnp.float32
BF16 = jnp.bfloat16
HIGHEST = lax.Precision.HIGHEST

EPS = 1e-6
GRID_W = 64
POOL_GROUPS = 4
SSD_HEAD_DIM = 64
SSD_GROUPS = 8
SSD_REP = 4
SSD_STATE = 128
SSD_CONV = 5
SSD_CHUNK = 128
PEER_HEADS = 8
PEER_NKEYS = 128
PEER_TOPK = 16

LANES = 128
VMEM_CAP = 56 * 1024 * 1024


def _params(sem, vmem_bytes):
    return pltpu.CompilerParams(dimension_semantics=sem,
                                vmem_limit_bytes=int(min(max(vmem_bytes, 16 << 20), VMEM_CAP)))


def _dot(a, b):
    return jnp.dot(a, b, preferred_element_type=F32)


def _dot_nt(a, b):
    return lax.dot_general(a, b, (((1,), (1,)), ((), ())), preferred_element_type=F32)


def _silu(x):
    return x * jax.nn.sigmoid(x)


def _mod_kernel(c_ref, w_ref, b_ref, o_ref):
    s = _silu(c_ref[...]).astype(BF16)
    o_ref[...] = _dot(s, w_ref[...].astype(BF16)) + b_ref[...]


def _mod(cin, w, b):
    rows, d = cin.shape
    n = w.shape[1]
    tn = 512
    return pl.pallas_call(
        _mod_kernel,
        grid=(n // tn,),
        in_specs=[pl.BlockSpec((rows, d), lambda j: (0, 0)),
                  pl.BlockSpec((d, tn), lambda j: (0, j)),
                  pl.BlockSpec((1, tn), lambda j: (0, j))],
        out_specs=pl.BlockSpec((rows, tn), lambda j: (0, j)),
        out_shape=jax.ShapeDtypeStruct((rows, n), F32),
        compiler_params=_params(("arbitrary",), 2 * d * tn * 4 + (4 << 20)),
        name="mod",
    )(cin, w, b.reshape(1, n))


def _nmm_kernel(*refs, has_dt, emit_h):
    x_ref, g_ref, sc_ref, sh_ref, w_ref = refs[:5]
    k = 5
    wdt_ref = None
    if has_dt:
        wdt_ref = refs[k]
        k += 1
    o_ref = refs[k]
    k += 1
    dt_ref = None
    if has_dt:
        dt_ref = refs[k]
        k += 1
    hout_ref = None
    if emit_h:
        hout_ref = refs[k]
        k += 1
    h_ref = refs[k]

    @pl.when(pl.program_id(1) == 0)
    def _():
        x = x_ref[...]
        ms = jnp.mean(x * x, axis=-1, keepdims=True)
        y = x * lax.rsqrt(ms + EPS) * g_ref[...]
        h = (y * (1.0 + sc_ref[...]) + sh_ref[...]).astype(BF16)
        h_ref[...] = h
        if has_dt:
            dt_ref[...] = _dot(h, wdt_ref[...])
        if emit_h:
            hout_ref[...] = h

    o_ref[...] = _dot(h_ref[...], w_ref[...]).astype(o_ref.dtype)


def _nmm(x2d, gain, sc, sh, w, *, col0, ncols, tok_per_batch, out_dtype, w_dt=None, emit_h=False,
         tm=512, tn=512, name="nmm"):
    t, d = x2d.shape
    nb = sc.shape[0]
    assert t % tm == 0 and ncols % tn == 0 and col0 % tn == 0 and tok_per_batch % tm == 0
    jb0 = col0 // tn
    if nb == 1:
        bmap = lambda i, j: (0, 0, 0)
    else:
        bpb = tok_per_batch // tm
        bmap = lambda i, j: (i // bpb, 0, 0)
    in_specs = [pl.BlockSpec((tm, d), lambda i, j: (i, 0)),
                pl.BlockSpec((1, d), lambda i, j: (0, 0)),
                pl.BlockSpec((None, 1, d), bmap),
                pl.BlockSpec((None, 1, d), bmap),
                pl.BlockSpec((d, tn), lambda i, j: (0, j + jb0))]
    args = [x2d, gain.reshape(1, d), sc, sh, w]
    out_specs = [pl.BlockSpec((tm, tn), lambda i, j: (i, j))]
    out_shape = [jax.ShapeDtypeStruct((t, ncols), out_dtype)]
    has_dt = w_dt is not None
    if has_dt:
        in_specs.append(pl.BlockSpec((d, LANES), lambda i, j: (0, 0)))
        args.append(w_dt)
        out_specs.append(pl.BlockSpec((tm, LANES), lambda i, j: (i, 0)))
        out_shape.append(jax.ShapeDtypeStruct((t, LANES), F32))
    if emit_h:
        out_specs.append(pl.BlockSpec((tm, d), lambda i, j: (i, 0)))
        out_shape.append(jax.ShapeDtypeStruct((t, d), BF16))
    vmem = 2 * tm * d * 4 + 2 * d * tn * 2 + tm * d * 2 + 2 * tm * tn * 4 + (2 * tm * d * 2 if emit_h else 0) + (10 << 20)
    return pl.pallas_call(
        functools.partial(_nmm_kernel, has_dt=has_dt, emit_h=emit_h),
        grid=(t // tm, ncols // tn),
        in_specs=in_specs,
        out_specs=out_specs,
        out_shape=out_shape,
        scratch_shapes=[pltpu.VMEM((tm, d), BF16)],
        compiler_params=_params(("parallel", "arbitrary"), vmem),
        name=name,
    )(*args)


def _softplus(x):
    return jnp.maximum(x, 0.0) + jnp.log1p(jnp.exp(-jnp.abs(x)))


def _expand_heads(cols, first):
    rows = cols.shape[0]
    lane = lax.broadcasted_iota(jnp.int32, (rows, LANES), 1)
    parts = []
    for q in range(2):
        lo = cols[:, first + 2 * q:first + 2 * q + 1]
        hi = cols[:, first + 2 * q + 1:first + 2 * q + 2]
        parts.append(jnp.where(lane < SSD_HEAD_DIM, lo, hi))
    return jnp.concatenate(parts, axis=1)


def _ssd_kernel(xh_ref, bm_ref, cm_ref, cwx_ref, cwb_ref, cwc_ref, dtr_ref, prm_ref, dsk_ref, h0_ref,
                *rest, n_tok, with_output):
    if with_output:
        y_ref, xc_ref, ext_ref, st_ref, dec_ref = rest
        hout_ref = None
    else:
        hout_ref, xc_ref, ext_ref, st_ref, dec_ref = rest
        y_ref = None
    q = SSD_CHUNK
    nc = n_tok // q
    hw = SSD_REP * SSD_HEAD_DIM
    halo = 16

    def conv_body(c, carry):
        r0 = pl.multiple_of(c * q, q)
        rp = pl.multiple_of(jnp.maximum(c * q - halo, 0), halo)
        rn = pl.multiple_of(jnp.minimum(c * q + q, n_tok - halo), halo)
        keep_p = (c > 0).astype(F32)
        keep_n = (c < nc - 1).astype(F32)
        for ref, cw_ref, l0, wd in ((xh_ref, cwx_ref, 0, hw), (bm_ref, cwb_ref, hw, SSD_STATE),
                                    (cm_ref, cwc_ref, hw + SSD_STATE, SSD_STATE)):
            ext_ref[0:halo, l0:l0 + wd] = ref[pl.ds(rp, halo), :].astype(F32) * keep_p
            ext_ref[halo:halo + q, l0:l0 + wd] = ref[pl.ds(r0, q), :].astype(F32)
            ext_ref[halo + q:2 * halo + q, l0:l0 + wd] = ref[pl.ds(rn, halo), :].astype(F32) * keep_n
            cw = cw_ref[...]
            acc = jnp.zeros((q, wd), F32) + cw[SSD_CONV:SSD_CONV + 1, :]
            for k in range(SSD_CONV):
                off = halo - SSD_CONV // 2 + k
                acc = acc + cw[k:k + 1, :] * ext_ref[off:off + q, l0:l0 + wd]
            xc_ref[pl.ds(r0, q), l0:l0 + wd] = _silu(acc).astype(BF16)
        return carry

    lax.fori_loop(0, nc, conv_body, 0)

    prm = prm_ref[...]
    bias_r = prm[:, 0:1]
    a_r = -jnp.exp(prm[:, 1:2])
    jj = lax.broadcasted_iota(jnp.int32, (q, q), 0)
    ii = lax.broadcasted_iota(jnp.int32, (q, q), 1)
    tri_f = (jj <= ii).astype(F32)
    tri_b = (jj >= ii).astype(F32)
    row8 = lax.broadcasted_iota(jnp.int32, (8, q), 0)

    def chunk_decay(c):
        r0 = pl.multiple_of(c * q, q)
        raw = dtr_ref[:, pl.ds(r0, q)]
        dt = _softplus(raw + bias_r)
        x = dt * a_r
        cs_f = jnp.dot(x, tri_f, precision=HIGHEST, preferred_element_type=F32)
        cs_b = jnp.dot(x, tri_b, precision=HIGHEST, preferred_element_type=F32)
        cs = jnp.where(row8 < SSD_REP, cs_f, cs_b)
        total = jnp.sum(x, axis=1, keepdims=True)
        return dt, cs, total

    def to_cols(*rows8):
        pad = jnp.zeros((q - 8 * len(rows8), q), F32)
        return jnp.concatenate(list(rows8) + [pad], axis=0).T

    def state_body(c, carry):
        r0 = pl.multiple_of(c * q, q)
        dt, cs, total = chunk_decay(c)
        wgt = jnp.exp(total - cs) * dt
        dec = jnp.broadcast_to(jnp.exp(total), (8, q))
        cols = to_cols(wgt, dec)
        xc = xc_ref[pl.ds(r0, q), :]
        xh = xc[:, 0:hw].astype(F32)
        xw = jnp.concatenate([xh * _expand_heads(cols, 0), xh * _expand_heads(cols, SSD_REP)],
                             axis=1).astype(BF16)
        bm_t = xc[:, hw:hw + SSD_STATE].astype(F32).T.astype(BF16)
        st_ref[c] = _dot(bm_t, xw)
        dec_ref[c] = jnp.concatenate([_expand_heads(cols, 8), _expand_heads(cols, 8 + SSD_REP)],
                                     axis=1)[0:8, :]
        return carry

    lax.fori_loop(0, nc, state_body, 0)

    def rec_body(i, carry):
        hf, hb = carry
        cf = i
        cb = nc - 1 - i
        sf = st_ref[cf, :, 0:hw]
        st_ref[cf, :, 0:hw] = hf
        hf = hf * dec_ref[cf, 0:1, 0:hw] + sf
        sb = st_ref[cb, :, hw:2 * hw]
        st_ref[cb, :, hw:2 * hw] = hb
        hb = hb * dec_ref[cb, 0:1, hw:2 * hw] + sb
        return hf, hb

    h0 = h0_ref[...]
    hf, hb = lax.fori_loop(0, nc, rec_body, (h0[:, 0:hw], h0[:, hw:2 * hw]))

    if not with_output:
        hout_ref[:, 0:hw] = hf
        hout_ref[:, hw:2 * hw] = hb
        return

    lane_head = jnp.right_shift(lax.broadcasted_iota(jnp.int32, (q, hw), 1), 6)
    dsk = dsk_ref[...]

    def out_body(c, carry):
        r0 = pl.multiple_of(c * q, q)
        dt, cs, total = chunk_decay(c)
        cols = to_cols(cs, jnp.exp(cs))
        xc = xc_ref[pl.ds(r0, q), :]
        xh_b = xc[:, 0:hw]
        bm = xc[:, hw:hw + SSD_STATE]
        cm = xc[:, hw + SSD_STATE:hw + 2 * SSD_STATE]
        cbm = _dot_nt(cm, bm)
        ydiag = jnp.zeros((q, hw), F32)
        for r in range(SSD_REP):
            seg_f = cols[:, r:r + 1] - cs[r:r + 1, :]
            l_f = jnp.exp(jnp.where(ii <= jj, seg_f, -jnp.inf)) * dt[r:r + 1, :]
            rb = SSD_REP + r
            seg_b = cols[:, rb:rb + 1] - cs[rb:rb + 1, :]
            l_b = jnp.exp(jnp.where(ii >= jj, seg_b, -jnp.inf)) * dt[rb:rb + 1, :]
            m = (cbm * (l_f + l_b)).astype(BF16)
            ydiag = ydiag + jnp.where(lane_head == r, _dot(m, xh_b), 0.0)
        st = st_ref[c]
        yo_f = _dot(cm, st[:, 0:hw].astype(BF16)) * _expand_heads(cols, 8)
        yo_b = _dot(cm, st[:, hw:2 * hw].astype(BF16)) * _expand_heads(cols, 8 + SSD_REP)
        y_ref[pl.ds(r0, q), :] = (ydiag + yo_f + yo_b + dsk * xh_b.astype(F32)).astype(y_ref.dtype)
        return carry

    lax.fori_loop(0, nc, out_body, 0)


def _ssd(xbc, col0, dtr, conv_wb, prm, dsk, h0, *, with_output):
    b, n, _ = xbc.shape
    g = SSD_GROUPS
    hw = SSD_REP * SSD_HEAD_DIM
    inner = g * hw
    assert col0 % hw == 0 and n % SSD_CHUNK == 0
    xb0 = col0 // hw
    bb0 = (col0 + inner) // SSD_STATE
    cb0 = (col0 + inner + g * SSD_STATE) // SSD_STATE
    nc = n // SSD_CHUNK
    in_specs = [
        pl.BlockSpec((None, n, hw), lambda bi, gi: (bi, 0, xb0 + gi)),
        pl.BlockSpec((None, n, SSD_STATE), lambda bi, gi: (bi, 0, bb0 + gi)),
        pl.BlockSpec((None, n, SSD_STATE), lambda bi, gi: (bi, 0, cb0 + gi)),
        pl.BlockSpec((8, hw), lambda bi, gi: (0, gi)),
        pl.BlockSpec((8, SSD_STATE), lambda bi, gi: (0, inner // SSD_STATE + gi)),
        pl.BlockSpec((8, SSD_STATE), lambda bi, gi: (0, (inner + g * SSD_STATE) // SSD_STATE + gi)),
        pl.BlockSpec((None, None, 8, n), lambda bi, gi: (bi, gi, 0, 0)),
        pl.BlockSpec((None, 8, 2), lambda bi, gi: (gi, 0, 0)),
        pl.BlockSpec((None, 1, hw), lambda bi, gi: (gi, 0, 0)),
        pl.BlockSpec((None, None, SSD_STATE, 2 * hw), lambda bi, gi: (bi, gi, 0, 0)),
    ]
    if with_output:
        out_specs = pl.BlockSpec((None, n, hw), lambda bi, gi: (bi, 0, gi))
        out_shape = jax.ShapeDtypeStruct((b, n, inner), F32)
    else:
        out_specs = pl.BlockSpec((None, None, SSD_STATE, 2 * hw), lambda bi, gi: (bi, gi, 0, 0))
        out_shape = jax.ShapeDtypeStruct((b, g, SSD_STATE, 2 * hw), F32)
    vmem = (2 * n * 2 * hw * 2 + n * 2 * hw * 2 + nc * SSD_STATE * 2 * hw * 4 + 2 * n * hw * 4
            + 2 * 8 * n * 4 + (8 << 20))
    return pl.pallas_call(
        functools.partial(_ssd_kernel, n_tok=n, with_output=with_output),
        grid=(b, g),
        in_specs=in_specs,
        out_specs=out_specs,
        out_shape=out_shape,
        scratch_shapes=[pltpu.VMEM((n, 2 * hw), BF16),
                        pltpu.VMEM((SSD_CHUNK + 32, 2 * hw), F32),
                        pltpu.VMEM((nc, SSD_STATE, 2 * hw), F32),
                        pltpu.VMEM((nc, 8, 2 * hw), F32)],
        compiler_params=_params(("parallel", "arbitrary"), vmem),
        name="ssd_out" if with_output else "ssd_state",
    )(xbc, xbc, xbc, conv_wb, conv_wb, conv_wb, dtr, prm, dsk, h0)


def _pool_kernel(v_ref, w_ref, sc_ref, o_ref, rm_ref, *, rows):
    k = pl.program_id(1)
    win = jnp.left_shift(2, k)
    half = win // 2
    gc = v_ref.shape[1]

    def row_body(r, carry):
        lo = jnp.maximum(r - half, 0)
        hi = jnp.minimum(r - half + win, rows)

        def acc_body(rr, acc):
            return acc + v_ref[pl.ds(pl.multiple_of(rr * GRID_W, GRID_W), GRID_W), :].astype(F32)

        tot = lax.fori_loop(lo, hi, acc_body, jnp.zeros((GRID_W, gc), F32))
        cnt = (hi - lo).astype(F32)
        rm_ref[pl.ds(pl.multiple_of(r * GRID_W, GRID_W), GRID_W), :] = (tot / cnt).astype(BF16)
        return carry

    lax.fori_loop(0, rows, row_body, 0)

    blk = 2 * GRID_W
    ti = lax.broadcasted_iota(jnp.int32, (blk, blk), 0)
    tj = lax.broadcasted_iota(jnp.int32, (blk, blk), 1)
    ci = ti & (GRID_W - 1)
    cj = tj & (GRID_W - 1)
    same_row = jnp.right_shift(ti, 6) == jnp.right_shift(tj, 6)
    band = jnp.where(same_row & (cj >= ci - half) & (cj < ci - half + win), 1.0, 0.0).astype(BF16)
    cc = lax.broadcasted_iota(jnp.int32, (blk, gc), 0) & (GRID_W - 1)
    cnt_c = (jnp.minimum(cc - half + win, GRID_W) - jnp.maximum(cc - half, 0)).astype(F32)
    w = w_ref[...]
    scale = sc_ref[...]

    def col_body(t, carry):
        r0 = pl.multiple_of(t * blk, blk)
        m = _dot(band, rm_ref[pl.ds(r0, blk), :]) / cnt_c
        p = (m - v_ref[pl.ds(r0, blk), :].astype(F32)).astype(BF16)
        o_ref[pl.ds(r0, blk), :] = (_dot(p, w) * scale).astype(o_ref.dtype)
        return carry

    lax.fori_loop(0, (rows * GRID_W) // blk, col_body, 0)


def _pool(pm, pool_w, pool_scale):
    b, n, _ = pm.shape
    gc = pool_w.shape[1]
    rows = n // GRID_W
    vmem = 2 * n * gc * 2 + n * gc * 2 + 2 * n * gc * 2 + 2 * gc * gc * 2 + (8 << 20)
    return pl.pallas_call(
        functools.partial(_pool_kernel, rows=rows),
        grid=(b, POOL_GROUPS),
        in_specs=[pl.BlockSpec((None, n, gc), lambda bi, ki: (bi, 0, ki)),
                  pl.BlockSpec((None, gc, gc), lambda bi, ki: (ki, 0, 0)),
                  pl.BlockSpec((None, 1, gc), lambda bi, ki: (ki, 0, 0))],
        out_specs=pl.BlockSpec((None, n, gc), lambda bi, ki: (bi, 0, ki)),
        out_shape=jax.ShapeDtypeStruct((b, n, POOL_GROUPS * gc), BF16),
        scratch_shapes=[pltpu.VMEM((n, gc), BF16)],
        compiler_params=_params(("parallel", "arbitrary"), vmem),
        name="pool",
    )(pm, pool_w, pool_scale)


def _oproj_kernel(pool_ref, y_ref, z_ref, nrm_ref, w_ref, x_ref, g_ref, o_ref, a_ref):
    pw = pool_ref.shape[1]

    @pl.when(pl.program_id(1) == 0)
    def _():
        yz = y_ref[...] * _silu(z_ref[...].astype(F32))
        ms = jnp.mean(yz * yz, axis=-1, keepdims=True)
        a_ref[:, 0:pw] = pool_ref[...]
        a_ref[:, pw:] = (yz * lax.rsqrt(ms + EPS) * nrm_ref[...]).astype(BF16)

    o_ref[...] = x_ref[...] + g_ref[...] * _dot(a_ref[...], w_ref[...])


def _oproj(pooled, y, pm, z_col0, ssd_norm, w_out, x2d, gate, tok_per_batch, tm=512, tn=512):
    t, d = x2d.shape
    pw = pooled.shape[1]
    sw = y.shape[1]
    assert z_col0 % sw == 0
    zb = z_col0 // sw
    bpb = tok_per_batch // tm
    vmem = 2 * tm * (pw * 2 + sw * 4 + sw * 2) + 2 * (pw + sw) * tn * 2 + tm * (pw + sw) * 2 + 4 * tm * tn * 4 + (6 << 20)
    return pl.pallas_call(
        _oproj_kernel,
        grid=(t // tm, d // tn),
        in_specs=[pl.BlockSpec((tm, pw), lambda i, j: (i, 0)),
                  pl.BlockSpec((tm, sw), lambda i, j: (i, 0)),
                  pl.BlockSpec((tm, sw), lambda i, j: (i, zb)),
                  pl.BlockSpec((1, sw), lambda i, j: (0, 0)),
                  pl.BlockSpec((pw + sw, tn), lambda i, j: (0, j)),
                  pl.BlockSpec((tm, tn), lambda i, j: (i, j)),
                  pl.BlockSpec((None, 1, tn), lambda i, j: (i // bpb, 0, j))],
        out_specs=pl.BlockSpec((tm, tn), lambda i, j: (i, j)),
        out_shape=jax.ShapeDtypeStruct((t, d), F32),
        scratch_shapes=[pltpu.VMEM((tm, pw + sw), BF16)],
        compiler_params=_params(("parallel", "arbitrary"), vmem),
        name="oproj",
    )(pooled, y, pm, ssd_norm.reshape(1, sw), w_out, x2d, gate)


def _extract_top(x, count):
    rows, t = x.shape
    ridx = lax.broadcasted_iota(jnp.int32, (rows, t), 0)
    slot = lax.broadcasted_iota(jnp.int32, (count, t), 0)

    def body(r, carry):
        xx, vals, idxs = carry
        m = jnp.max(xx, axis=0, keepdims=True)
        first = jnp.min(jnp.where(xx == m, ridx, rows), axis=0, keepdims=True)
        xx = jnp.where(ridx == first, -jnp.inf, xx)
        vals = jnp.where(slot == r, m, vals)
        idxs = jnp.where(slot == r, first, idxs)
        return xx, vals, idxs

    _, vals, idxs = lax.fori_loop(0, count, body,
                                  (x, jnp.zeros((count, t), F32), jnp.zeros((count, t), jnp.int32)))
    return vals, idxs


def _router_kernel(q_ref, k_ref, r2_ref, e2_ref, l1_ref, g1_ref):
    nk = PEER_NKEYS
    kk = PEER_TOPK
    q = q_ref[...]
    half = q.shape[1] // 2
    s1 = _dot_nt(k_ref[0], q[:, 0:half].astype(BF16))
    s2 = _dot_nt(k_ref[1], q[:, half:].astype(BF16))
    t = s1.shape[1]
    a, ia = _extract_top(s1, kk)
    b, ib = _extract_top(s2, kk)
    cand = jnp.concatenate([a[r:r + 1, :] + b for r in range(kk)], axis=0)
    cv, ci = _extract_top(cand, kk)
    zsum = jnp.sum(jnp.exp(cv - cv[0:1, :]), axis=0, keepdims=True)
    slot = lax.broadcasted_iota(jnp.int32, (kk, t), 0)
    r1 = jnp.right_shift(ci, 4)
    cnt = jnp.zeros((kk, t), F32)
    for r in range(kk):
        cnt = cnt + (slot == r1[r:r + 1, :]).astype(F32)
    krow = lax.broadcasted_iota(jnp.int32, (nk, t), 0)
    rank2 = jnp.full((nk, t), float(kk), F32)
    lim1 = jnp.zeros((nk, t), F32)
    for r in range(kk):
        rank2 = jnp.where(krow == ib[r:r + 1, :], float(r), rank2)
        lim1 = jnp.where(krow == ia[r:r + 1, :], cnt[r:r + 1, :], lim1)
    r2_ref[...] = rank2
    l1_ref[...] = lim1
    e2_ref[...] = jnp.exp(s2 - b[0:1, :])
    g1_ref[...] = jnp.exp(s1 - a[0:1, :]) / zsum


def _router(qry, keys, tm=256):
    t, qw = qry.shape
    hq = qw // PEER_HEADS
    nk = PEER_NKEYS
    spec_o = pl.BlockSpec((None, nk, tm), lambda i, h: (h, 0, i))
    shp = jax.ShapeDtypeStruct((PEER_HEADS, nk, t), F32)
    return pl.pallas_call(
        _router_kernel,
        grid=(t // tm, PEER_HEADS),
        in_specs=[pl.BlockSpec((tm, hq), lambda i, h: (i, h)),
                  pl.BlockSpec((None, 2, nk, hq // 2), lambda i, h: (h, 0, 0, 0))],
        out_specs=[spec_o, spec_o, spec_o, spec_o],
        out_shape=[shp, shp, shp, shp],
        compiler_params=_params(("parallel", "arbitrary"), 32 << 20),
        name="router",
    )(qry, keys)


def _gelu(x):
    return 0.5 * x * (1.0 + lax.erf(x * (2.0 ** -0.5)))


def _ffn_kernel(h_ref, u_ref, vt_ref, r2_ref, e2_ref, l1_ref, g1_ref, o_ref, gt_ref, *, te, tc):
    e = pl.program_id(1)
    nk = PEER_NKEYS
    tm = h_ref.shape[0]
    at = _dot_nt(u_ref[...], h_ref[...])
    for s in range(te // nk):
        k1 = e * (te // nk) + s
        for c0 in range(0, tm, tc):
            w = jnp.zeros((nk, tc), F32)
            for hd in range(PEER_HEADS):
                lim = l1_ref[hd, pl.ds(k1, 1), c0:c0 + tc]
                gsc = g1_ref[hd, pl.ds(k1, 1), c0:c0 + tc]
                sel = r2_ref[hd, :, c0:c0 + tc] < lim
                w = w + jnp.where(sel, e2_ref[hd, :, c0:c0 + tc], 0.0) * gsc
            act = _gelu(at[s * nk:(s + 1) * nk, c0:c0 + tc])
            gt_ref[s * nk:(s + 1) * nk, c0:c0 + tc] = (act * w).astype(BF16)
    @pl.when(e == 0)
    def _():
        o_ref[...] = jnp.zeros_like(o_ref)

    d = vt_ref.shape[0]
    rc = 512
    for r0 in range(0, d, rc):
        o_ref[r0:r0 + rc, :] += _dot(vt_ref[r0:r0 + rc, :], gt_ref[...])


def _ffn(h2, u, vt, r2, e2, l1, g1, tm=512, te=512, tc=256):
    t, d = h2.shape
    ne = u.shape[0]
    nk = PEER_NKEYS
    rspec = pl.BlockSpec((PEER_HEADS, nk, tm), lambda i, e: (0, 0, i), pipeline_mode=pl.Buffered(1))
    vmem = tm * d * 2 + 4 * PEER_HEADS * nk * tm * 4 + 2 * 2 * te * d * 2 + 2 * d * tm * 4 + te * tm * 6 + (6 << 20)
    return pl.pallas_call(
        functools.partial(_ffn_kernel, te=te, tc=tc),
        grid=(t // tm, ne // te),
        in_specs=[pl.BlockSpec((tm, d), lambda i, e: (i, 0), pipeline_mode=pl.Buffered(1)),
                  pl.BlockSpec((te, d), lambda i, e: (e, 0)),
                  pl.BlockSpec((d, te), lambda i, e: (0, e)),
                  rspec, rspec, rspec, rspec],
        out_specs=pl.BlockSpec((d, tm), lambda i, e: (0, i)),
        out_shape=jax.ShapeDtypeStruct((d, t), F32),
        scratch_shapes=[pltpu.VMEM((te, tm), BF16)],
        compiler_params=_params(("parallel", "arbitrary"), vmem),
        name="peer_ffn",
    )(h2, u, vt, r2, e2, l1, g1)


def _final_kernel(x_ref, ft_ref, g_ref, n_ref, o_ref):
    x = x_ref[...] + g_ref[...] * ft_ref[...].T
    ms = jnp.mean(x * x, axis=-1, keepdims=True)
    o_ref[...] = x * lax.rsqrt(ms + EPS) * n_ref[...]


def _final(x2d, ffn_t, gate, final_norm, tok_per_batch, tm=256):
    t, d = x2d.shape
    bpb = tok_per_batch // tm
    return pl.pallas_call(
        _final_kernel,
        grid=(t // tm,),
        in_specs=[pl.BlockSpec((tm, d), lambda i: (i, 0)),
                  pl.BlockSpec((d, tm), lambda i: (0, i)),
                  pl.BlockSpec((None, 1, d), lambda i: (i // bpb, 0, 0)),
                  pl.BlockSpec((1, d), lambda i: (0, 0))],
        out_specs=pl.BlockSpec((tm, d), lambda i: (i, 0)),
        out_shape=jax.ShapeDtypeStruct((t, d), F32),
        compiler_params=_params(("parallel",), 6 * tm * d * 4 + (8 << 20)),
        name="final",
    )(x2d, ffn_t, gate, final_norm.reshape(1, d))


def kernel(x, c, ctx, c_ctx, w_mod, b_mod, norm1, norm2, w_in, pool_w, pool_scale, conv_w, conv_b, dt_bias, a_log,
           d_skip, ssd_norm, w_out, peer_wq, peer_keys, peer_u, peer_v, final_norm):
    assert w_mod.shape[0] == 1, "single-layer trunk"
    bsz, n, d = x.shape
    n_ctx = ctx.shape[1]
    g = SSD_GROUPS
    inner = g * SSD_REP * SSD_HEAD_DIM
    gn = g * SSD_STATE
    pool_wd = POOL_GROUPS * pool_w.shape[2]
    off_z = pool_wd
    off_xbc = off_z + inner
    off_dt = off_xbc + inner + 2 * gn
    heads = g * SSD_REP

    cin = jnp.concatenate([c, c_ctx[None, :], jnp.zeros((8 - bsz - 1, d), F32)], axis=0)
    mod = _mod(cin, w_mod[0], b_mod[0])
    lat = mod[:bsz].reshape(bsz, 6, 1, d)
    sh1, sc1, g1, sh2, sc2, g2 = (lat[:, k] for k in range(6))
    cmod = mod[bsz].reshape(6, 1, 1, d)
    csh1, csc1 = cmod[0], cmod[1]

    w_in_b = w_in[0].astype(BF16)
    w_main = w_in_b[:, :off_dt]
    w_dt = jnp.pad(w_in_b[:, off_dt:], ((0, 0), (0, LANES - 2 * heads)))

    conv_wb = jnp.concatenate([conv_w[0], conv_b[0][None, :],
                               jnp.zeros((8 - SSD_CONV - 1, conv_w.shape[2]), F32)], axis=0)
    prm = jnp.stack([dt_bias[0].reshape(2, g, SSD_REP).transpose(1, 0, 2).reshape(g, 8),
                     a_log[0].reshape(2, g, SSD_REP).transpose(1, 0, 2).reshape(g, 8)], axis=-1)
    dsk = jnp.repeat(d_skip[0], SSD_HEAD_DIM).reshape(g, 1, SSD_REP * SSD_HEAD_DIM)

    def dt_rows(dt_out, nb, nt):
        r = dt_out[:, :2 * heads].reshape(nb, nt, 2, g, SSD_REP)
        return r.transpose(0, 3, 2, 4, 1).reshape(nb, g, 8, nt)

    pc, dtc = _nmm(ctx.reshape(bsz * n_ctx, d), norm1[0], csc1, csh1, w_main, col0=off_xbc, ncols=off_dt - off_xbc,
                   tok_per_batch=n_ctx, out_dtype=BF16, w_dt=w_dt, tm=256, name="inproj_ctx")
    zeros_h = jnp.zeros((bsz, g, SSD_STATE, 2 * SSD_REP * SSD_HEAD_DIM), F32)
    h_ctx = _ssd(pc.reshape(bsz, n_ctx, -1), 0, dt_rows(dtc, bsz, n_ctx), conv_wb, prm, dsk, zeros_h,
                 with_output=False)

    x2d = x.reshape(bsz * n, d)
    pm, dtl = _nmm(x2d, norm1[0], sc1, sh1, w_main, col0=0, ncols=off_dt, tok_per_batch=n, out_dtype=BF16,
                   w_dt=w_dt, name="inproj")
    pm3 = pm.reshape(bsz, n, off_dt)
    y = _ssd(pm3, off_xbc, dt_rows(dtl, bsz, n), conv_wb, prm, dsk, h_ctx, with_output=True)
    pooled = _pool(pm3, pool_w[0].astype(BF16), pool_scale[0].reshape(POOL_GROUPS, 1, -1))
    x1 = _oproj(pooled.reshape(bsz * n, pool_wd), y.reshape(bsz * n, inner), pm, off_z, ssd_norm[0],
                w_out[0].astype(BF16), x2d, g1, n)

    qry, h2 = _nmm(x1, norm2[0], sc2, sh2, peer_wq[0].astype(BF16), col0=0, ncols=peer_wq.shape[2],
                   tok_per_batch=n, out_dtype=F32, emit_h=True, name="peer_q")
    keys = peer_keys[0].astype(BF16)
    r2, e2, l1, gg = _router(qry, keys)
    ffn_t = _ffn(h2, peer_u[0].astype(BF16), peer_v[0].T.astype(BF16), r2, e2, l1, gg)
    out = _final(x1, ffn_t, g2, final_norm, n)
    return out.reshape(bsz, n, d)
```

```python
import functools

import jax
import jax.numpy as jnp
from jax import lax
from jax.experimental import pallas as pl
from jax.experimental.pallas import tpu as pltpu

F32 = jnp.float32
BF16 = jnp.bfloat16
HIGHEST = lax.Precision.HIGHEST

EPS = 1e-6
GRID_W = 64
POOL_GROUPS = 4
SSD_HEAD_DIM = 64
SSD_GROUPS = 8
SSD_REP = 4
SSD_STATE = 128
SSD_CONV = 5
SSD_CHUNK = 128
PEER_HEADS = 8
PEER_NKEYS = 128
PEER_TOPK = 16

LANES = 128
BF16_SUBLANES = 16
VMEM_CAP = 56 * 1024 * 1024


def _params(sem, vmem_bytes):
    return pltpu.CompilerParams(dimension_semantics=sem,
                                vmem_limit_bytes=int(min(max(vmem_bytes, 16 << 20), VMEM_CAP)))


def _dot(a, b):
    return jnp.dot(a, b, preferred_element_type=F32)


def _dot_nt(a, b):
    return lax.dot_general(a, b, (((1,), (1,)), ((), ())), preferred_element_type=F32)


def _silu(x):
    return x * jax.nn.sigmoid(x)


def _mod_kernel(c_ref, w_ref, b_ref, o_ref):
    s = _silu(c_ref[...]).astype(BF16)
    o_ref[...] = _dot(s, w_ref[...].astype(BF16)) + b_ref[...]


def _mod(cin, w, b):
    rows, d = cin.shape
    n = w.shape[1]
    tn = 512
    return pl.pallas_call(
        _mod_kernel,
        grid=(n // tn,),
        in_specs=[pl.BlockSpec((rows, d), lambda j: (0, 0)),
                  pl.BlockSpec((d, tn), lambda j: (0, j)),
                  pl.BlockSpec((1, tn), lambda j: (0, j))],
        out_specs=pl.BlockSpec((rows, tn), lambda j: (0, j)),
        out_shape=jax.ShapeDtypeStruct((rows, n), F32),
        compiler_params=_params(("arbitrary",), 2 * d * tn * 4 + (4 << 20)),
        name="mod",
    )(cin, w, b.reshape(1, n))


def _nmm_kernel(*refs, has_dt, emit_ht):
    x_ref, g_ref, sc_ref, sh_ref, w_ref = refs[:5]
    k = 5
    wdt_ref = None
    if has_dt:
        wdt_ref = refs[k]
        k += 1
    o_ref = refs[k]
    k += 1
    dt_ref = None
    if has_dt:
        dt_ref = refs[k]
        k += 1
    ht_ref = None
    if emit_ht:
        ht_ref = refs[k]
        k += 1
    h_ref = refs[k]

    @pl.when(pl.program_id(1) == 0)
    def _():
        x = x_ref[...]
        ms = jnp.mean(x * x, axis=-1, keepdims=True)
        y = x * lax.rsqrt(ms + EPS) * g_ref[...]
        h32 = y * (1.0 + sc_ref[...]) + sh_ref[...]
        h = h32.astype(BF16)
        h_ref[...] = h
        if has_dt:
            dt_ref[...] = _dot(h, wdt_ref[...])
        if emit_ht:
            for r0 in range(0, h32.shape[0], LANES):
                ht_ref[:, r0:r0 + LANES] = h32[r0:r0 + LANES, :].T.astype(BF16)

    o_ref[...] = _dot(h_ref[...], w_ref[...]).astype(o_ref.dtype)


def _nmm(x2d, gain, sc, sh, w, *, col0, ncols, tok_per_batch, out_dtype, w_dt=None, emit_ht=False,
         tm=512, tn=512, name="nmm"):
    t, d = x2d.shape
    nb = sc.shape[0]
    assert t % tm == 0 and ncols % tn == 0 and col0 % tn == 0 and tok_per_batch % tm == 0
    jb0 = col0 // tn
    if nb == 1:
        bmap = lambda i, j: (0, 0, 0)
    else:
        bpb = tok_per_batch // tm
        bmap = lambda i, j: (i // bpb, 0, 0)
    in_specs = [pl.BlockSpec((tm, d), lambda i, j: (i, 0)),
                pl.BlockSpec((1, d), lambda i, j: (0, 0)),
                pl.BlockSpec((None, 1, d), bmap),
                pl.BlockSpec((None, 1, d), bmap),
                pl.BlockSpec((d, tn), lambda i, j: (0, j + jb0))]
    args = [x2d, gain.reshape(1, d), sc, sh, w]
    out_specs = [pl.BlockSpec((tm, tn), lambda i, j: (i, j))]
    out_shape = [jax.ShapeDtypeStruct((t, ncols), out_dtype)]
    has_dt = w_dt is not None
    if has_dt:
        in_specs.append(pl.BlockSpec((d, LANES), lambda i, j: (0, 0)))
        args.append(w_dt)
        out_specs.append(pl.BlockSpec((tm, LANES), lambda i, j: (i, 0)))
        out_shape.append(jax.ShapeDtypeStruct((t, LANES), F32))
    if emit_ht:
        out_specs.append(pl.BlockSpec((d, tm), lambda i, j: (0, i)))
        out_shape.append(jax.ShapeDtypeStruct((d, t), BF16))
    vmem = (2 * tm * d * 4 + 2 * d * tn * 2 + tm * d * 2 + 2 * tm * tn * 4 + (2 * tm * d * 2 if emit_ht else 0)
            + (10 << 20))
    return pl.pallas_call(
        functools.partial(_nmm_kernel, has_dt=has_dt, emit_ht=emit_ht),
        grid=(t // tm, ncols // tn),
        in_specs=in_specs,
        out_specs=out_specs,
        out_shape=out_shape,
        scratch_shapes=[pltpu.VMEM((tm, d), BF16)],
        compiler_params=_params(("parallel", "arbitrary"), vmem),
        name=name,
    )(*args)


def _softplus(x):
    return jnp.maximum(x, 0.0) + jnp.log1p(jnp.exp(-jnp.abs(x)))


def _expand_heads(cols, first):
    rows = cols.shape[0]
    lane = lax.broadcasted_iota(jnp.int32, (rows, LANES), 1)
    parts = []
    for q in range(2):
        lo = cols[:, first + 2 * q:first + 2 * q + 1]
        hi = cols[:, first + 2 * q + 1:first + 2 * q + 2]
        parts.append(jnp.where(lane < SSD_HEAD_DIM, lo, hi))
    return jnp.concatenate(parts, axis=1)


def _ssd_kernel(xh_ref, bm_ref, cm_ref, cwx_ref, cwb_ref, cwc_ref, dtr_ref, prm_ref, dsk_ref, h0_ref,
                *rest, n_tok, with_output):
    if with_output:
        y_ref, xc_ref, ext_ref, st_ref, dec_ref, rows_ref, cols_ref = rest
        hout_ref = None
    else:
        hout_ref, xc_ref, ext_ref, st_ref, dec_ref, rows_ref, cols_ref = rest
        y_ref = None
    q = SSD_CHUNK
    nc = n_tok // q
    hw = SSD_REP * SSD_HEAD_DIM
    halo = 16

    def conv_body(c, carry):
        r0 = pl.multiple_of(c * q, q)
        rp = pl.multiple_of(jnp.maximum(c * q - halo, 0), halo)
        rn = pl.multiple_of(jnp.minimum(c * q + q, n_tok - halo), halo)
        keep_p = (c > 0).astype(F32)
        keep_n = (c < nc - 1).astype(F32)
        for ref, cw_ref, l0, wd in ((xh_ref, cwx_ref, 0, hw), (bm_ref, cwb_ref, hw, SSD_STATE),
                                    (cm_ref, cwc_ref, hw + SSD_STATE, SSD_STATE)):
            ext_ref[0:halo, l0:l0 + wd] = ref[pl.ds(rp, halo), :].astype(F32) * keep_p
            ext_ref[halo:halo + q, l0:l0 + wd] = ref[pl.ds(r0, q), :].astype(F32)
            ext_ref[halo + q:2 * halo + q, l0:l0 + wd] = ref[pl.ds(rn, halo), :].astype(F32) * keep_n
            cw = cw_ref[...]
            acc = jnp.zeros((q, wd), F32) + cw[SSD_CONV:SSD_CONV + 1, :]
            for k in range(SSD_CONV):
                off = halo - SSD_CONV // 2 + k
                acc = acc + cw[k:k + 1, :] * ext_ref[off:off + q, l0:l0 + wd]
            xc_ref[pl.ds(r0, q), l0:l0 + wd] = _silu(acc).astype(BF16)
        return carry

    lax.fori_loop(0, nc, conv_body, 0)

    prm = prm_ref[...]
    bias_r = prm[:, 0:1]
    a_r = -jnp.exp(prm[:, 1:2])
    dt_all = _softplus(dtr_ref[...] + bias_r)
    x_all = dt_all * a_r
    for c in range(nc):
        rows_ref[0, 8 * c:8 * c + 8, :] = x_all[:, c * q:(c + 1) * q]
        rows_ref[1, 8 * c:8 * c + 8, :] = dt_all[:, c * q:(c + 1) * q]
    xs = rows_ref[0]
    dts = rows_ref[1]
    jj = lax.broadcasted_iota(jnp.int32, (q, q), 0)
    ii = lax.broadcasted_iota(jnp.int32, (q, q), 1)
    tri_f = (jj <= ii).astype(F32)
    tri_b = (jj >= ii).astype(F32)
    cs_f = jnp.dot(xs, tri_f, precision=HIGHEST, preferred_element_type=F32)
    cs_b = jnp.dot(xs, tri_b, precision=HIGHEST, preferred_element_type=F32)
    rowk = lax.broadcasted_iota(jnp.int32, (8 * nc, q), 0) & 7
    cs = jnp.where(rowk < SSD_REP, cs_f, cs_b)
    total = jnp.sum(xs, axis=1, keepdims=True)
    wgt = jnp.exp(total - cs) * dts
    ecs = jnp.exp(cs)
    dec = jnp.broadcast_to(jnp.exp(total), (8 * nc, q))
    rows_ref[0] = cs
    pad = jnp.zeros((q - 32, q), F32)
    for c in range(nc):
        sl = slice(8 * c, 8 * c + 8)
        cols_ref[c] = jnp.concatenate([cs[sl], wgt[sl], ecs[sl], dec[sl], pad], axis=0).T

    def state_body(c, carry):
        r0 = pl.multiple_of(c * q, q)
        cols = cols_ref[c]
        xc = xc_ref[pl.ds(r0, q), :]
        xh = xc[:, 0:hw].astype(F32)
        xw = jnp.concatenate([xh * _expand_heads(cols, 8), xh * _expand_heads(cols, 8 + SSD_REP)],
                             axis=1).astype(BF16)
        bm_t = xc[:, hw:hw + SSD_STATE].astype(F32).T.astype(BF16)
        st_ref[c] = _dot(bm_t, xw)
        dec_ref[c] = jnp.concatenate([_expand_heads(cols, 24), _expand_heads(cols, 24 + SSD_REP)],
                                     axis=1)[0:8, :]
        return carry

    lax.fori_loop(0, nc, state_body, 0, unroll=2)

    def rec_body(i, carry):
        hf, hb = carry
        cf = i
        cb = nc - 1 - i
        sf = st_ref[cf, :, 0:hw]
        st_ref[cf, :, 0:hw] = hf
        hf = hf * dec_ref[cf, 0:1, 0:hw] + sf
        sb = st_ref[cb, :, hw:2 * hw]
        st_ref[cb, :, hw:2 * hw] = hb
        hb = hb * dec_ref[cb, 0:1, hw:2 * hw] + sb
        return hf, hb

    h0 = h0_ref[...]
    hf, hb = lax.fori_loop(0, nc, rec_body, (h0[:, 0:hw], h0[:, hw:2 * hw]))

    if not with_output:
        hout_ref[:, 0:hw] = hf
        hout_ref[:, hw:2 * hw] = hb
        return

    lane_head = jnp.right_shift(lax.broadcasted_iota(jnp.int32, (q, hw), 1), 6)
    dsk = dsk_ref[...]

    def out_body(c, carry):
        r0 = pl.multiple_of(c * q, q)
        r8 = pl.multiple_of(c * 8, 8)
        cols = cols_ref[c]
        cs_r = rows_ref[0, pl.ds(r8, 8), :]
        dt_r = rows_ref[1, pl.ds(r8, 8), :]
        xc = xc_ref[pl.ds(r0, q), :]
        xh_b = xc[:, 0:hw]
        bm = xc[:, hw:hw + SSD_STATE]
        cm = xc[:, hw + SSD_STATE:hw + 2 * SSD_STATE]
        cbm = _dot_nt(cm, bm)
        ydiag = jnp.zeros((q, hw), F32)
        for r in range(SSD_REP):
            seg_f = cols[:, r:r + 1] - cs_r[r:r + 1, :]
            l_f = jnp.exp(jnp.where(ii <= jj, seg_f, -jnp.inf)) * dt_r[r:r + 1, :]
            rb = SSD_REP + r
            seg_b = cols[:, rb:rb + 1] - cs_r[rb:rb + 1, :]
            l_b = jnp.exp(jnp.where(ii >= jj, seg_b, -jnp.inf)) * dt_r[rb:rb + 1, :]
            m = (cbm * (l_f + l_b)).astype(BF16)
            ydiag = ydiag + jnp.where(lane_head == r, _dot(m, xh_b), 0.0)
        st = st_ref[c]
        yo_f = _dot(cm, st[:, 0:hw].astype(BF16)) * _expand_heads(cols, 16)
        yo_b = _dot(cm, st[:, hw:2 * hw].astype(BF16)) * _expand_heads(cols, 16 + SSD_REP)
        y_ref[pl.ds(r0, q), :] = (ydiag + yo_f + yo_b + dsk * xh_b.astype(F32)).astype(y_ref.dtype)
        return carry

    lax.fori_loop(0, nc, out_body, 0, unroll=2)


def _ssd(xbc, col0, dtr, conv_wb, prm, dsk, h0, *, with_output):
    b, n, _ = xbc.shape
    g = SSD_GROUPS
    hw = SSD_REP * SSD_HEAD_DIM
    inner = g * hw
    assert col0 % hw == 0 and n % SSD_CHUNK == 0
    xb0 = col0 // hw
    bb0 = (col0 + inner) // SSD_STATE
    cb0 = (col0 + inner + g * SSD_STATE) // SSD_STATE
    nc = n // SSD_CHUNK
    in_specs = [
        pl.BlockSpec((None, n, hw), lambda bi, gi: (bi, 0, xb0 + gi)),
        pl.BlockSpec((None, n, SSD_STATE), lambda bi, gi: (bi, 0, bb0 + gi)),
        pl.BlockSpec((None, n, SSD_STATE), lambda bi, gi: (bi, 0, cb0 + gi)),
        pl.BlockSpec((8, hw), lambda bi, gi: (0, gi)),
        pl.BlockSpec((8, SSD_STATE), lambda bi, gi: (0, inner // SSD_STATE + gi)),
        pl.BlockSpec((8, SSD_STATE), lambda bi, gi: (0, (inner + g * SSD_STATE) // SSD_STATE + gi)),
        pl.BlockSpec((None, None, 8, n), lambda bi, gi: (bi, gi, 0, 0)),
        pl.BlockSpec((None, 8, 2), lambda bi, gi: (gi, 0, 0)),
        pl.BlockSpec((None, 1, hw), lambda bi, gi: (gi, 0, 0)),
        pl.BlockSpec((None, None, SSD_STATE, 2 * hw), lambda bi, gi: (bi, gi, 0, 0)),
    ]
    if with_output:
        out_specs = pl.BlockSpec((None, n, hw), lambda bi, gi: (bi, 0, gi))
        out_shape = jax.ShapeDtypeStruct((b, n, inner), BF16)
    else:
        out_specs = pl.BlockSpec((None, None, SSD_STATE, 2 * hw), lambda bi, gi: (bi, gi, 0, 0))
        out_shape = jax.ShapeDtypeStruct((b, g, SSD_STATE, 2 * hw), F32)
    vmem = (2 * n * 2 * hw * 2 + n * 2 * hw * 2 + nc * SSD_STATE * 2 * hw * 4 + 2 * n * hw * 4
            + 2 * 8 * n * 4 + nc * SSD_CHUNK * SSD_CHUNK * 4 + (8 << 20))
    return pl.pallas_call(
        functools.partial(_ssd_kernel, n_tok=n, with_output=with_output),
        grid=(b, g),
        in_specs=in_specs,
        out_specs=out_specs,
        out_shape=out_shape,
        scratch_shapes=[pltpu.VMEM((n, 2 * hw), BF16),
                        pltpu.VMEM((SSD_CHUNK + 32, 2 * hw), F32),
                        pltpu.VMEM((nc, SSD_STATE, 2 * hw), F32),
                        pltpu.VMEM((nc, 8, 2 * hw), F32),
                        pltpu.VMEM((2, 8 * nc, SSD_CHUNK), F32),
                        pltpu.VMEM((nc, SSD_CHUNK, SSD_CHUNK), F32)],
        compiler_params=_params(("parallel", "arbitrary"), vmem),
        name="ssd_out" if with_output else "ssd_state",
    )(xbc, xbc, xbc, conv_wb, conv_wb, conv_wb, dtr, prm, dsk, h0)


def _pool_kernel(v_ref, w_ref, sc_ref, o_ref, rm_ref, *, rows):
    k = pl.program_id(1)
    win = jnp.left_shift(2, k)
    half = win // 2
    gc = v_ref.shape[1]

    def row_body(r, carry):
        lo = jnp.maximum(r - half, 0)
        hi = jnp.minimum(r - half + win, rows)

        def acc_body(rr, acc):
            return acc + v_ref[pl.ds(pl.multiple_of(rr * GRID_W, GRID_W), GRID_W), :].astype(F32)

        tot = lax.fori_loop(lo, hi, acc_body, jnp.zeros((GRID_W, gc), F32))
        cnt = (hi - lo).astype(F32)
        rm_ref[pl.ds(pl.multiple_of(r * GRID_W, GRID_W), GRID_W), :] = (tot / cnt).astype(BF16)
        return carry

    lax.fori_loop(0, rows, row_body, 0)

    blk = 2 * GRID_W
    ti = lax.broadcasted_iota(jnp.int32, (blk, blk), 0)
    tj = lax.broadcasted_iota(jnp.int32, (blk, blk), 1)
    ci = ti & (GRID_W - 1)
    cj = tj & (GRID_W - 1)
    same_row = jnp.right_shift(ti, 6) == jnp.right_shift(tj, 6)
    band = jnp.where(same_row & (cj >= ci - half) & (cj < ci - half + win), 1.0, 0.0).astype(BF16)
    cc = lax.broadcasted_iota(jnp.int32, (blk, gc), 0) & (GRID_W - 1)
    cnt_c = (jnp.minimum(cc - half + win, GRID_W) - jnp.maximum(cc - half, 0)).astype(F32)
    w = w_ref[...]
    scale = sc_ref[...]

    def col_body(t, carry):
        r0 = pl.multiple_of(t * blk, blk)
        m = _dot(band, rm_ref[pl.ds(r0, blk), :]) / cnt_c
        p = (m - v_ref[pl.ds(r0, blk), :].astype(F32)).astype(BF16)
        o_ref[pl.ds(r0, blk), :] = (_dot(p, w) * scale).astype(o_ref.dtype)
        return carry

    lax.fori_loop(0, (rows * GRID_W) // blk, col_body, 0, unroll=2)


def _pool(pm, pool_w, pool_scale):
    b, n, _ = pm.shape
    gc = pool_w.shape[1]
    rows = n // GRID_W
    vmem = 2 * n * gc * 2 + n * gc * 2 + 2 * n * gc * 2 + 2 * gc * gc * 2 + (8 << 20)
    return pl.pallas_call(
        functools.partial(_pool_kernel, rows=rows),
        grid=(b, POOL_GROUPS),
        in_specs=[pl.BlockSpec((None, n, gc), lambda bi, ki: (bi, 0, ki)),
                  pl.BlockSpec((None, gc, gc), lambda bi, ki: (ki, 0, 0)),
                  pl.BlockSpec((None, 1, gc), lambda bi, ki: (ki, 0, 0))],
        out_specs=pl.BlockSpec((None, n, gc), lambda bi, ki: (bi, 0, ki)),
        out_shape=jax.ShapeDtypeStruct((b, n, POOL_GROUPS * gc), BF16),
        scratch_shapes=[pltpu.VMEM((n, gc), BF16)],
        compiler_params=_params(("parallel", "arbitrary"), vmem),
        name="pool",
    )(pm, pool_w, pool_scale)


def _oproj_kernel(pool_ref, y_ref, z_ref, nrm_ref, w_ref, x_ref, g_ref, o_ref, a_ref):
    pw = pool_ref.shape[1]

    @pl.when(pl.program_id(1) == 0)
    def _():
        yz = y_ref[...].astype(F32) * _silu(z_ref[...].astype(F32))
        ms = jnp.mean(yz * yz, axis=-1, keepdims=True)
        a_ref[:, 0:pw] = pool_ref[...]
        a_ref[:, pw:] = (yz * lax.rsqrt(ms + EPS) * nrm_ref[...]).astype(BF16)

    o_ref[...] = x_ref[...] + g_ref[...] * _dot(a_ref[...], w_ref[...])


def _oproj(pooled, y, pm, z_col0, ssd_norm, w_out, x2d, gate, tok_per_batch, tm=512, tn=1024):
    t, d = x2d.shape
    pw = pooled.shape[1]
    sw = y.shape[1]
    assert z_col0 % sw == 0
    zb = z_col0 // sw
    bpb = tok_per_batch // tm
    vmem = 2 * tm * (pw * 2 + sw * 2 + sw * 2) + 2 * (pw + sw) * tn * 2 + tm * (pw + sw) * 2 + 4 * tm * tn * 4 + (8 << 20)
    return pl.pallas_call(
        _oproj_kernel,
        grid=(t // tm, d // tn),
        in_specs=[pl.BlockSpec((tm, pw), lambda i, j: (i, 0)),
                  pl.BlockSpec((tm, sw), lambda i, j: (i, 0)),
                  pl.BlockSpec((tm, sw), lambda i, j: (i, zb)),
                  pl.BlockSpec((1, sw), lambda i, j: (0, 0)),
                  pl.BlockSpec((pw + sw, tn), lambda i, j: (0, j)),
                  pl.BlockSpec((tm, tn), lambda i, j: (i, j)),
                  pl.BlockSpec((None, 1, tn), lambda i, j: (i // bpb, 0, j))],
        out_specs=pl.BlockSpec((tm, tn), lambda i, j: (i, j)),
        out_shape=jax.ShapeDtypeStruct((t, d), F32),
        scratch_shapes=[pltpu.VMEM((tm, pw + sw), BF16)],
        compiler_params=_params(("parallel", "arbitrary"), vmem),
        name="oproj",
    )(pooled, y, pm, ssd_norm.reshape(1, sw), w_out, x2d, gate)


def _extract_top(xs, ridxs, count, sentinel):
    t = xs[0].shape[1]
    slot = lax.broadcasted_iota(jnp.int32, (count, t), 0)

    def body(r, carry):
        out = []
        for (xx, vals, idxs), ridx in zip(carry, ridxs):
            m = jnp.max(xx, axis=0, keepdims=True)
            first = jnp.min(jnp.where(xx == m, ridx, sentinel), axis=0, keepdims=True)
            xx = jnp.where(ridx == first, -jnp.inf, xx)
            vals = jnp.where(slot == r, m, vals)
            idxs = jnp.where(slot == r, first, idxs)
            out.append((xx, vals, idxs))
        return tuple(out)

    init = tuple((x, jnp.zeros((count, t), F32), jnp.zeros((count, t), jnp.int32)) for x in xs)
    res = lax.fori_loop(0, count, body, init)
    return [(vals, idxs) for _, vals, idxs in res]


def _router_kernel(q_ref, k_ref, r2_ref, e2_ref, l1_ref, g1_ref):
    nk = PEER_NKEYS
    kk = PEER_TOPK
    q = q_ref[...]
    half = q.shape[1] // 2
    s1 = _dot_nt(k_ref[0], q[:, 0:half].astype(BF16))
    s2 = _dot_nt(k_ref[1], q[:, half:].astype(BF16))
    t = s1.shape[1]
    krow = lax.broadcasted_iota(jnp.int32, (nk, t), 0)
    (a, ia), (b, ib) = _extract_top([s1, s2], [krow, krow], kk, nk)
    cand = jnp.concatenate([a[0:1, :] + b] + [a[r:r + 1, :] + b[0:8, :] for r in range(1, 8)]
                           + [a[8:16, :] + b[0:1, :]], axis=0)
    crow = lax.broadcasted_iota(jnp.int32, (80, t), 0)
    c_r1 = jnp.where(crow < 16, 0, jnp.where(crow < 72, jnp.right_shift(crow - 8, 3), crow - 64))
    c_r2 = jnp.where(crow < 16, crow, jnp.where(crow < 72, crow & 7, 0))
    flat = c_r1 * kk + c_r2
    ((cv, ci),) = _extract_top([cand], [flat], kk, kk * kk)
    zsum = jnp.sum(jnp.exp(cv - cv[0:1, :]), axis=0, keepdims=True)
    slot = lax.broadcasted_iota(jnp.int32, (kk, t), 0)
    r1 = jnp.right_shift(ci, 4)
    cnt = jnp.zeros((kk, t), F32)
    for r in range(kk):
        cnt = cnt + (slot == r1[r:r + 1, :]).astype(F32)
    rank2 = jnp.full((nk, t), float(kk), F32)
    lim1 = jnp.zeros((nk, t), F32)
    for r in range(kk):
        rank2 = jnp.where(krow == ib[r:r + 1, :], float(r), rank2)
        lim1 = jnp.where(krow == ia[r:r + 1, :], cnt[r:r + 1, :], lim1)
    r2_ref[...] = rank2.astype(BF16)
    l1_ref[...] = lim1
    e2_ref[...] = jnp.exp(s2 - b[0:1, :]).astype(BF16)
    g1_ref[...] = jnp.exp(s1 - a[0:1, :]) / zsum


def _router(qry, keys, tm=256):
    t, qw = qry.shape
    hq = qw // PEER_HEADS
    nk = PEER_NKEYS
    spec_o = pl.BlockSpec((None, nk, tm), lambda i, h: (h, 0, i))
    shp_b = jax.ShapeDtypeStruct((PEER_HEADS, nk, t), BF16)
    shp_f = jax.ShapeDtypeStruct((PEER_HEADS, nk, t), F32)
    return pl.pallas_call(
        _router_kernel,
        grid=(t // tm, PEER_HEADS),
        in_specs=[pl.BlockSpec((tm, hq), lambda i, h: (i, h)),
                  pl.BlockSpec((None, 2, nk, hq // 2), lambda i, h: (h, 0, 0, 0))],
        out_specs=[spec_o, spec_o, spec_o, spec_o],
        out_shape=[shp_b, shp_b, shp_f, shp_f],
        compiler_params=_params(("parallel", "arbitrary"), 32 << 20),
        name="router",
    )(qry, keys)


def _gelu(x):
    return 0.5 * x * (1.0 + lax.erf(x * (2.0 ** -0.5)))


def _ffn_kernel(ht_ref, u_ref, vt_ref, r2_ref, e2_ref, l1_ref, g1_ref, o_ref, *, te, tc, rc):
    e = pl.program_id(1)
    nk = PEER_NKEYS
    sub = BF16_SUBLANES
    d, tm = ht_ref.shape
    ns = te // nk
    nt = tm // tc

    @pl.when(e == 0)
    def _():
        o_ref[...] = jnp.zeros_like(o_ref)

    def scores(ti):
        return _dot(u_ref[...], ht_ref[:, ti * tc:(ti + 1) * tc])

    def gates(ti):
        c0 = ti * tc
        tiles = []
        for s in range(ns):
            k1 = e * ns + s
            lims = []
            gscs = []
            for hd in range(PEER_HEADS):
                lims.append(jnp.broadcast_to(l1_ref[hd, pl.ds(k1, 1), c0:c0 + tc], (sub, tc)).astype(BF16))
                gscs.append(jnp.broadcast_to(g1_ref[hd, pl.ds(k1, 1), c0:c0 + tc], (sub, tc)).astype(BF16))
            for j in range(nk // sub):
                w = jnp.zeros((sub, tc), BF16)
                for hd in range(PEER_HEADS):
                    r2 = r2_ref[hd, j * sub:(j + 1) * sub, c0:c0 + tc]
                    e2 = e2_ref[hd, j * sub:(j + 1) * sub, c0:c0 + tc]
                    w = w + jnp.where(r2 < lims[hd], e2, jnp.zeros_like(e2)) * gscs[hd]
                tiles.append(w)
        return jnp.concatenate(tiles, axis=0)

    def accumulate(ti, g):
        c0 = ti * tc
        for r0 in range(0, d, rc):
            o_ref[r0:r0 + rc, c0:c0 + tc] += _dot(vt_ref[r0:r0 + rc, :], g)

    at = scores(0)
    w = gates(0)
    for ti in range(nt):
        at_next = scores(ti + 1) if ti + 1 < nt else None
        g = (_gelu(at) * w.astype(F32)).astype(BF16)
        w_next = gates(ti + 1) if ti + 1 < nt else None
        accumulate(ti, g)
        at, w = at_next, w_next


def _ffn(ht, u, vt, r2, e2, l1, g1, tm=512, te=512, tc=256, rc=1024):
    d, t = ht.shape
    ne = u.shape[0]
    nk = PEER_NKEYS
    once = pl.Buffered(1)
    rspec = pl.BlockSpec((PEER_HEADS, nk, tm), lambda i, e: (0, 0, i), pipeline_mode=once)
    vmem = (d * tm * 2 + 2 * PEER_HEADS * nk * tm * (2 + 4) + 2 * 2 * te * d * 2 + 2 * d * tm * 4
            + 2 * te * tc * 4 + rc * tc * 4 + (6 << 20))
    return pl.pallas_call(
        functools.partial(_ffn_kernel, te=te, tc=tc, rc=rc),
        grid=(t // tm, ne // te),
        in_specs=[pl.BlockSpec((d, tm), lambda i, e: (0, i), pipeline_mode=once),
                  pl.BlockSpec((te, d), lambda i, e: (e, 0)),
                  pl.BlockSpec((d, te), lambda i, e: (0, e)),
                  rspec, rspec, rspec, rspec],
        out_specs=pl.BlockSpec((d, tm), lambda i, e: (0, i)),
        out_shape=jax.ShapeDtypeStruct((d, t), F32),
        compiler_params=_params(("parallel", "arbitrary"), vmem),
        name="peer_ffn",
    )(ht, u, vt, r2, e2, l1, g1)


def _final_kernel(x_ref, ft_ref, g_ref, n_ref, o_ref):
    x = x_ref[...] + g_ref[...] * ft_ref[...].T
    ms = jnp.mean(x * x, axis=-1, keepdims=True)
    o_ref[...] = x * lax.rsqrt(ms + EPS) * n_ref[...]


def _final(x2d, ffn_t, gate, final_norm, tok_per_batch, tm=256):
    t, d = x2d.shape
    bpb = tok_per_batch // tm
    return pl.pallas_call(
        _final_kernel,
        grid=(t // tm,),
        in_specs=[pl.BlockSpec((tm, d), lambda i: (i, 0)),
                  pl.BlockSpec((d, tm), lambda i: (0, i)),
                  pl.BlockSpec((None, 1, d), lambda i: (i // bpb, 0, 0)),
                  pl.BlockSpec((1, d), lambda i: (0, 0))],
        out_specs=pl.BlockSpec((tm, d), lambda i: (i, 0)),
        out_shape=jax.ShapeDtypeStruct((t, d), F32),
        compiler_params=_params(("parallel",), 6 * tm * d * 4 + (8 << 20)),
        name="final",
    )(x2d, ffn_t, gate, final_norm.reshape(1, d))


def kernel(x, c, ctx, c_ctx, w_mod, b_mod, norm1, norm2, w_in, pool_w, pool_scale, conv_w, conv_b, dt_bias, a_log,
           d_skip, ssd_norm, w_out, peer_wq, peer_keys, peer_u, peer_v, final_norm):
    assert w_mod.shape[0] == 1, "single-layer trunk"
    bsz, n, d = x.shape
    n_ctx = ctx.shape[1]
    g = SSD_GROUPS
    inner = g * SSD_REP * SSD_HEAD_DIM
    gn = g * SSD_STATE
    pool_wd = POOL_GROUPS * pool_w.shape[2]
    off_z = pool_wd
    off_xbc = off_z + inner
    off_dt = off_xbc + inner + 2 * gn
    heads = g * SSD_REP

    cin = jnp.concatenate([c, c_ctx[None, :], jnp.zeros((8 - bsz - 1, d), F32)], axis=0)
    mod = _mod(cin, w_mod[0], b_mod[0])
    lat = mod[:bsz].reshape(bsz, 6, 1, d)
    sh1, sc1, g1, sh2, sc2, g2 = (lat[:, k] for k in range(6))
    cmod = mod[bsz].reshape(6, 1, 1, d)
    csh1, csc1 = cmod[0], cmod[1]

    w_in_b = w_in[0].astype(BF16)
    w_dt = jnp.pad(w_in_b[:, off_dt:], ((0, 0), (0, LANES - 2 * heads)))

    conv_wb = jnp.concatenate([conv_w[0], conv_b[0][None, :],
                               jnp.zeros((8 - SSD_CONV - 1, conv_w.shape[2]), F32)], axis=0)
    prm = jnp.stack([dt_bias[0].reshape(2, g, SSD_REP).transpose(1, 0, 2).reshape(g, 8),
                     a_log[0].reshape(2, g, SSD_REP).transpose(1, 0, 2).reshape(g, 8)], axis=-1)
    dsk = jnp.repeat(d_skip[0], SSD_HEAD_DIM).reshape(g, 1, SSD_REP * SSD_HEAD_DIM)

    def dt_rows(dt_out, nb, nt):
        r = dt_out[:, :2 * heads].reshape(nb, nt, 2, g, SSD_REP)
        return r.transpose(0, 3, 2, 4, 1).reshape(nb, g, 8, nt)

    pc, dtc = _nmm(ctx.reshape(bsz * n_ctx, d), norm1[0], csc1, csh1, w_in_b, col0=off_xbc, ncols=off_dt - off_xbc,
                   tok_per_batch=n_ctx, out_dtype=BF16, w_dt=w_dt, tm=256, tn=1024, name="inproj_ctx")
    zeros_h = jnp.zeros((bsz, g, SSD_STATE, 2 * SSD_REP * SSD_HEAD_DIM), F32)
    h_ctx = _ssd(pc.reshape(bsz, n_ctx, -1), 0, dt_rows(dtc, bsz, n_ctx), conv_wb, prm, dsk, zeros_h,
                 with_output=False)

    x2d = x.reshape(bsz * n, d)
    pm, dtl = _nmm(x2d, norm1[0], sc1, sh1, w_in_b, col0=0, ncols=off_dt, tok_per_batch=n, out_dtype=BF16,
                   w_dt=w_dt, tn=1024, name="inproj")
    pm3 = pm.reshape(bsz, n, off_dt)
    y = _ssd(pm3, off_xbc, dt_rows(dtl, bsz, n), conv_wb, prm, dsk, h_ctx, with_output=True)
    pooled = _pool(pm3, pool_w[0].astype(BF16), pool_scale[0].reshape(POOL_GROUPS, 1, -1))
    x1 = _oproj(pooled.reshape(bsz * n, pool_wd), y.reshape(bsz * n, inner), pm, off_z, ssd_norm[0],
                w_out[0].astype(BF16), x2d, g1, n)

    qry, h2t = _nmm(x1, norm2[0], sc2, sh2, peer_wq[0].astype(BF16), col0=0, ncols=peer_wq.shape[2],
                    tok_per_batch=n, out_dtype=F32, emit_ht=True, name="peer_q")
    r2, e2, l1, gg = _router(qry, peer_keys[0].astype(BF16))
    ffn_t = _ffn(h2t, peer_u[0].astype(BF16), peer_v[0].astype(BF16).T, r2, e2, l1, gg)
    out = _final(x1, ffn_t, g2, final_norm, n)
    return out.reshape(bsz, n, d)
```

```python
import functools

import jax
import jax.numpy as jnp
from jax import lax
from jax.experimental import pallas as pl
from jax.experimental.pallas import tpu as pltpu

F32 = jnp.float32
BF16 = jnp.bfloat16
HIGHEST = lax.Precision.HIGHEST

EPS = 1e-6
GRID_W = 64
POOL_GROUPS = 4
SSD_HEAD_DIM = 64
SSD_GROUPS = 8
SSD_REP = 4
SSD_STATE = 128
SSD_CONV = 5
SSD_CHUNK = 128
PEER_HEADS = 8
PEER_NKEYS = 128
PEER_TOPK = 16

LANES = 128
BF16_SUBLANES = 16
VMEM_CAP = 56 * 1024 * 1024


def _params(sem, vmem_bytes):
    return pltpu.CompilerParams(dimension_semantics=sem,
                                vmem_limit_bytes=int(min(max(vmem_bytes, 16 << 20), VMEM_CAP)))


def _dot(a, b):
    return jnp.dot(a, b, preferred_element_type=F32)


def _dot_nt(a, b):
    return lax.dot_general(a, b, (((1,), (1,)), ((), ())), preferred_element_type=F32)


def _silu(x):
    return x * jax.nn.sigmoid(x)


def _mod_kernel(c_ref, w_ref, b_ref, o_ref):
    s = _silu(c_ref[...]).astype(BF16)
    o_ref[...] = _dot(s, w_ref[...].astype(BF16)) + b_ref[...]


def _mod(cin, w, b):
    rows, d = cin.shape
    n = w.shape[1]
    tn = 512
    return pl.pallas_call(
        _mod_kernel,
        grid=(n // tn,),
        in_specs=[pl.BlockSpec((rows, d), lambda j: (0, 0)),
                  pl.BlockSpec((d, tn), lambda j: (0, j)),
                  pl.BlockSpec((1, tn), lambda j: (0, j))],
        out_specs=pl.BlockSpec((rows, tn), lambda j: (0, j)),
        out_shape=jax.ShapeDtypeStruct((rows, n), F32),
        compiler_params=_params(("arbitrary",), 2 * d * tn * 4 + (4 << 20)),
        name="mod",
    )(cin, w, b.reshape(1, n))


def _nmm_kernel(*refs, has_dt, emit_ht):
    x_ref, g_ref, sc_ref, sh_ref, w_ref = refs[:5]
    k = 5
    wdt_ref = None
    if has_dt:
        wdt_ref = refs[k]
        k += 1
    o_ref = refs[k]
    k += 1
    dt_ref = None
    if has_dt:
        dt_ref = refs[k]
        k += 1
    ht_ref = None
    if emit_ht:
        ht_ref = refs[k]
        k += 1
    h_ref = refs[k]

    @pl.when(pl.program_id(1) == 0)
    def _():
        x = x_ref[...]
        ms = jnp.mean(x * x, axis=-1, keepdims=True)
        y = x * lax.rsqrt(ms + EPS) * g_ref[...]
        h32 = y * (1.0 + sc_ref[...]) + sh_ref[...]
        h = h32.astype(BF16)
        h_ref[...] = h
        if has_dt:
            dt_ref[...] = _dot(h, wdt_ref[...])
        if emit_ht:
            for r0 in range(0, h32.shape[0], LANES):
                ht_ref[:, r0:r0 + LANES] = h32[r0:r0 + LANES, :].T.astype(BF16)

    o_ref[...] = _dot(h_ref[...], w_ref[...]).astype(o_ref.dtype)


def _nmm(x2d, gain, sc, sh, w, *, col0, ncols, tok_per_batch, out_dtype, w_dt=None, emit_ht=False,
         tm=512, tn=512, name="nmm"):
    t, d = x2d.shape
    nb = sc.shape[0]
    assert t % tm == 0 and ncols % tn == 0 and col0 % tn == 0 and tok_per_batch % tm == 0
    jb0 = col0 // tn
    if nb == 1:
        bmap = lambda i, j: (0, 0, 0)
    else:
        bpb = tok_per_batch // tm
        bmap = lambda i, j: (i // bpb, 0, 0)
    in_specs = [pl.BlockSpec((tm, d), lambda i, j: (i, 0)),
                pl.BlockSpec((1, d), lambda i, j: (0, 0)),
                pl.BlockSpec((None, 1, d), bmap),
                pl.BlockSpec((None, 1, d), bmap),
                pl.BlockSpec((d, tn), lambda i, j: (0, j + jb0))]
    args = [x2d, gain.reshape(1, d), sc, sh, w]
    out_specs = [pl.BlockSpec((tm, tn), lambda i, j: (i, j))]
    out_shape = [jax.ShapeDtypeStruct((t, ncols), out_dtype)]
    has_dt = w_dt is not None
    if has_dt:
        in_specs.append(pl.BlockSpec((d, LANES), lambda i, j: (0, 0)))
        args.append(w_dt)
        out_specs.append(pl.BlockSpec((tm, LANES), lambda i, j: (i, 0)))
        out_shape.append(jax.ShapeDtypeStruct((t, LANES), F32))
    if emit_ht:
        out_specs.append(pl.BlockSpec((d, tm), lambda i, j: (0, i)))
        out_shape.append(jax.ShapeDtypeStruct((d, t), BF16))
    vmem = (2 * tm * d * 4 + 2 * d * tn * 2 + tm * d * 2 + 2 * tm * tn * 4 + (2 * tm * d * 2 if emit_ht else 0)
            + (10 << 20))
    return pl.pallas_call(
        functools.partial(_nmm_kernel, has_dt=has_dt, emit_ht=emit_ht),
        grid=(t // tm, ncols // tn),
        in_specs=in_specs,
        out_specs=out_specs,
        out_shape=out_shape,
        scratch_shapes=[pltpu.VMEM((tm, d), BF16)],
        compiler_params=_params(("parallel", "arbitrary"), vmem),
        name=name,
    )(*args)


def _softplus(x):
    return jnp.maximum(x, 0.0) + jnp.log1p(jnp.exp(-jnp.abs(x)))


def _expand_heads(cols, first):
    rows = cols.shape[0]
    lane = lax.broadcasted_iota(jnp.int32, (rows, LANES), 1)
    parts = []
    for q in range(2):
        lo = cols[:, first + 2 * q:first + 2 * q + 1]
        hi = cols[:, first + 2 * q + 1:first + 2 * q + 2]
        parts.append(jnp.where(lane < SSD_HEAD_DIM, lo, hi))
    return jnp.concatenate(parts, axis=1)


def _ssd_kernel(xh_ref, bm_ref, cm_ref, cwx_ref, cwb_ref, cwc_ref, dtr_ref, prm_ref, dsk_ref, h0_ref,
                *rest, n_tok, with_output):
    if with_output:
        y_ref, xc_ref, ext_ref, st_ref, dec_ref, rows_ref, cols_ref = rest
        hout_ref = None
    else:
        hout_ref, xc_ref, ext_ref, st_ref, dec_ref, rows_ref, cols_ref = rest
        y_ref = None
    q = SSD_CHUNK
    nc = n_tok // q
    hw = SSD_REP * SSD_HEAD_DIM
    halo = 16

    def conv_body(c, carry):
        r0 = pl.multiple_of(c * q, q)
        rp = pl.multiple_of(jnp.maximum(c * q - halo, 0), halo)
        rn = pl.multiple_of(jnp.minimum(c * q + q, n_tok - halo), halo)
        keep_p = jnp.where(c > 0, 1.0, 0.0)
        keep_n = jnp.where(c < nc - 1, 1.0, 0.0)
        for ref, cw_ref, l0, wd in ((xh_ref, cwx_ref, 0, hw), (bm_ref, cwb_ref, hw, SSD_STATE),
                                    (cm_ref, cwc_ref, hw + SSD_STATE, SSD_STATE)):
            ext_ref[0:halo, l0:l0 + wd] = ref[pl.ds(rp, halo), :].astype(F32) * keep_p
            ext_ref[halo:halo + q, l0:l0 + wd] = ref[pl.ds(r0, q), :].astype(F32)
            ext_ref[halo + q:2 * halo + q, l0:l0 + wd] = ref[pl.ds(rn, halo), :].astype(F32) * keep_n
            cw = cw_ref[...]
            acc = jnp.zeros((q, wd), F32) + cw[SSD_CONV:SSD_CONV + 1, :]
            for k in range(SSD_CONV):
                off = halo - SSD_CONV // 2 + k
                acc = acc + cw[k:k + 1, :] * ext_ref[off:off + q, l0:l0 + wd]
            xc_ref[pl.ds(r0, q), l0:l0 + wd] = _silu(acc).astype(BF16)
        return carry

    lax.fori_loop(0, nc, conv_body, 0)

    prm = prm_ref[...]
    bias_r = prm[:, 0:1]
    a_r = -jnp.exp(prm[:, 1:2])
    dt_all = _softplus(dtr_ref[...] + bias_r)
    x_all = dt_all * a_r
    for c in range(nc):
        rows_ref[0, 8 * c:8 * c + 8, :] = x_all[:, c * q:(c + 1) * q]
        rows_ref[1, 8 * c:8 * c + 8, :] = dt_all[:, c * q:(c + 1) * q]
    xs = rows_ref[0]
    dts = rows_ref[1]
    jj = lax.broadcasted_iota(jnp.int32, (q, q), 0)
    ii = lax.broadcasted_iota(jnp.int32, (q, q), 1)
    tri_f = (jj <= ii).astype(F32)
    tri_b = (jj >= ii).astype(F32)
    cs_f = jnp.dot(xs, tri_f, precision=HIGHEST, preferred_element_type=F32)
    cs_b = jnp.dot(xs, tri_b, precision=HIGHEST, preferred_element_type=F32)
    rowk = lax.broadcasted_iota(jnp.int32, (8 * nc, q), 0) & 7
    cs = jnp.where(rowk < SSD_REP, cs_f, cs_b)
    total = jnp.sum(xs, axis=1, keepdims=True)
    wgt = jnp.exp(total - cs) * dts
    ecs = jnp.exp(cs)
    dec = jnp.broadcast_to(jnp.exp(total), (8 * nc, q))
    rows_ref[0] = cs
    pad = jnp.zeros((q - 32, q), F32)
    for c in range(nc):
        sl = slice(8 * c, 8 * c + 8)
        cols_ref[c] = jnp.concatenate([cs[sl], wgt[sl], ecs[sl], dec[sl], pad], axis=0).T

    def state_body(c, carry):
        r0 = pl.multiple_of(c * q, q)
        cols = cols_ref[c]
        xc = xc_ref[pl.ds(r0, q), :]
        xh = xc[:, 0:hw].astype(F32)
        xw = jnp.concatenate([xh * _expand_heads(cols, 8), xh * _expand_heads(cols, 8 + SSD_REP)],
                             axis=1).astype(BF16)
        bm_t = xc[:, hw:hw + SSD_STATE].astype(F32).T.astype(BF16)
        st_ref[c] = _dot(bm_t, xw)
        dec_ref[c] = jnp.concatenate([_expand_heads(cols, 24), _expand_heads(cols, 24 + SSD_REP)],
                                     axis=1)[0:8, :]
        return carry

    lax.fori_loop(0, nc, state_body, 0, unroll=2)

    def rec_body(i, carry):
        hf, hb = carry
        cf = i
        cb = nc - 1 - i
        sf = st_ref[cf, :, 0:hw]
        st_ref[cf, :, 0:hw] = hf
        hf = hf * dec_ref[cf, 0:1, 0:hw] + sf
        sb = st_ref[cb, :, hw:2 * hw]
        st_ref[cb, :, hw:2 * hw] = hb
        hb = hb * dec_ref[cb, 0:1, hw:2 * hw] + sb
        return hf, hb

    h0 = h0_ref[...]
    hf, hb = lax.fori_loop(0, nc, rec_body, (h0[:, 0:hw], h0[:, hw:2 * hw]))

    if not with_output:
        hout_ref[:, 0:hw] = hf
        hout_ref[:, hw:2 * hw] = hb
        return

    lane_head = jnp.right_shift(lax.broadcasted_iota(jnp.int32, (q, hw), 1), 6)
    dsk = dsk_ref[...]

    def out_body(c, carry):
        r0 = pl.multiple_of(c * q, q)
        r8 = pl.multiple_of(c * 8, 8)
        cols = cols_ref[c]
        cs_r = rows_ref[0, pl.ds(r8, 8), :]
        dt_r = rows_ref[1, pl.ds(r8, 8), :]
        xc = xc_ref[pl.ds(r0, q), :]
        xh_b = xc[:, 0:hw]
        bm = xc[:, hw:hw + SSD_STATE]
        cm = xc[:, hw + SSD_STATE:hw + 2 * SSD_STATE]
        cbm = _dot_nt(cm, bm)
        ydiag = jnp.zeros((q, hw), F32)
        for r in range(SSD_REP):
            seg_f = cols[:, r:r + 1] - cs_r[r:r + 1, :]
            l_f = jnp.exp(jnp.where(ii <= jj, seg_f, -jnp.inf)) * dt_r[r:r + 1, :]
            rb = SSD_REP + r
            seg_b = cols[:, rb:rb + 1] - cs_r[rb:rb + 1, :]
            l_b = jnp.exp(jnp.where(ii >= jj, seg_b, -jnp.inf)) * dt_r[rb:rb + 1, :]
            m = (cbm * (l_f + l_b)).astype(BF16)
            ydiag = ydiag + jnp.where(lane_head == r, _dot(m, xh_b), 0.0)
        st = st_ref[c]
        yo_f = _dot(cm, st[:, 0:hw].astype(BF16)) * _expand_heads(cols, 16)
        yo_b = _dot(cm, st[:, hw:2 * hw].astype(BF16)) * _expand_heads(cols, 16 + SSD_REP)
        y_ref[pl.ds(r0, q), :] = (ydiag + yo_f + yo_b + dsk * xh_b.astype(F32)).astype(y_ref.dtype)
        return carry

    lax.fori_loop(0, nc, out_body, 0, unroll=2)


def _ssd(xbc, col0, dtr, conv_wb, prm, dsk, h0, *, with_output):
    b, n, _ = xbc.shape
    g = SSD_GROUPS
    hw = SSD_REP * SSD_HEAD_DIM
    inner = g * hw
    assert col0 % hw == 0 and n % SSD_CHUNK == 0
    xb0 = col0 // hw
    bb0 = (col0 + inner) // SSD_STATE
    cb0 = (col0 + inner + g * SSD_STATE) // SSD_STATE
    nc = n // SSD_CHUNK
    in_specs = [
        pl.BlockSpec((None, n, hw), lambda bi, gi: (bi, 0, xb0 + gi)),
        pl.BlockSpec((None, n, SSD_STATE), lambda bi, gi: (bi, 0, bb0 + gi)),
        pl.BlockSpec((None, n, SSD_STATE), lambda bi, gi: (bi, 0, cb0 + gi)),
        pl.BlockSpec((8, hw), lambda bi, gi: (0, gi)),
        pl.BlockSpec((8, SSD_STATE), lambda bi, gi: (0, inner // SSD_STATE + gi)),
        pl.BlockSpec((8, SSD_STATE), lambda bi, gi: (0, (inner + g * SSD_STATE) // SSD_STATE + gi)),
        pl.BlockSpec((None, None, 8, n), lambda bi, gi: (bi, gi, 0, 0)),
        pl.BlockSpec((None, 8, 2), lambda bi, gi: (gi, 0, 0)),
        pl.BlockSpec((None, 1, hw), lambda bi, gi: (gi, 0, 0)),
        pl.BlockSpec((None, None, SSD_STATE, 2 * hw), lambda bi, gi: (bi, gi, 0, 0)),
    ]
    if with_output:
        out_specs = pl.BlockSpec((None, n, hw), lambda bi, gi: (bi, 0, gi))
        out_shape = jax.ShapeDtypeStruct((b, n, inner), BF16)
    else:
        out_specs = pl.BlockSpec((None, None, SSD_STATE, 2 * hw), lambda bi, gi: (bi, gi, 0, 0))
        out_shape = jax.ShapeDtypeStruct((b, g, SSD_STATE, 2 * hw), F32)
    vmem = (2 * n * 2 * hw * 2 + n * 2 * hw * 2 + nc * SSD_STATE * 2 * hw * 4 + 2 * n * hw * 4
            + 2 * 8 * n * 4 + nc * SSD_CHUNK * SSD_CHUNK * 4 + (8 << 20))
    return pl.pallas_call(
        functools.partial(_ssd_kernel, n_tok=n, with_output=with_output),
        grid=(b, g),
        in_specs=in_specs,
        out_specs=out_specs,
        out_shape=out_shape,
        scratch_shapes=[pltpu.VMEM((n, 2 * hw), BF16),
                        pltpu.VMEM((SSD_CHUNK + 32, 2 * hw), F32),
                        pltpu.VMEM((nc, SSD_STATE, 2 * hw), F32),
                        pltpu.VMEM((nc, 8, 2 * hw), F32),
                        pltpu.VMEM((2, 8 * nc, SSD_CHUNK), F32),
                        pltpu.VMEM((nc, SSD_CHUNK, SSD_CHUNK), F32)],
        compiler_params=_params(("parallel", "arbitrary"), vmem),
        name="ssd_out" if with_output else "ssd_state",
    )(xbc, xbc, xbc, conv_wb, conv_wb, conv_wb, dtr, prm, dsk, h0)


def _pool_kernel(v_ref, w_ref, sc_ref, o_ref, rm_ref, *, rows):
    k = pl.program_id(1)
    win = jnp.left_shift(2, k)
    half = win // 2
    gc = v_ref.shape[1]

    def row_body(r, carry):
        lo = jnp.maximum(r - half, 0)
        hi = jnp.minimum(r - half + win, rows)

        def acc_body(rr, acc):
            return acc + v_ref[pl.ds(pl.multiple_of(rr * GRID_W, GRID_W), GRID_W), :].astype(F32)

        tot = lax.fori_loop(lo, hi, acc_body, jnp.zeros((GRID_W, gc), F32))
        cnt = (hi - lo).astype(F32)
        rm_ref[pl.ds(pl.multiple_of(r * GRID_W, GRID_W), GRID_W), :] = (tot / cnt).astype(BF16)
        return carry

    lax.fori_loop(0, rows, row_body, 0)

    blk = 2 * GRID_W
    ti = lax.broadcasted_iota(jnp.int32, (blk, blk), 0)
    tj = lax.broadcasted_iota(jnp.int32, (blk, blk), 1)
    ci = ti & (GRID_W - 1)
    cj = tj & (GRID_W - 1)
    same_row = jnp.right_shift(ti, 6) == jnp.right_shift(tj, 6)
    band = jnp.where(same_row & (cj >= ci - half) & (cj < ci - half + win), 1.0, 0.0).astype(BF16)
    cc = lax.broadcasted_iota(jnp.int32, (blk, gc), 0) & (GRID_W - 1)
    cnt_c = (jnp.minimum(cc - half + win, GRID_W) - jnp.maximum(cc - half, 0)).astype(F32)
    w = w_ref[...]
    scale = sc_ref[...]

    def col_body(t, carry):
        r0 = pl.multiple_of(t * blk, blk)
        m = _dot(band, rm_ref[pl.ds(r0, blk), :]) / cnt_c
        p = (m - v_ref[pl.ds(r0, blk), :].astype(F32)).astype(BF16)
        o_ref[pl.ds(r0, blk), :] = (_dot(p, w) * scale).astype(o_ref.dtype)
        return carry

    lax.fori_loop(0, (rows * GRID_W) // blk, col_body, 0, unroll=2)


def _pool(pm, pool_w, pool_scale):
    b, n, _ = pm.shape
    gc = pool_w.shape[1]
    rows = n // GRID_W
    vmem = 2 * n * gc * 2 + n * gc * 2 + 2 * n * gc * 2 + 2 * gc * gc * 2 + (8 << 20)
    return pl.pallas_call(
        functools.partial(_pool_kernel, rows=rows),
        grid=(b, POOL_GROUPS),
        in_specs=[pl.BlockSpec((None, n, gc), lambda bi, ki: (bi, 0, ki)),
                  pl.BlockSpec((None, gc, gc), lambda bi, ki: (ki, 0, 0)),
                  pl.BlockSpec((None, 1, gc), lambda bi, ki: (ki, 0, 0))],
        out_specs=pl.BlockSpec((None, n, gc), lambda bi, ki: (bi, 0, ki)),
        out_shape=jax.ShapeDtypeStruct((b, n, POOL_GROUPS * gc), BF16),
        scratch_shapes=[pltpu.VMEM((n, gc), BF16)],
        compiler_params=_params(("parallel", "arbitrary"), vmem),
        name="pool",
    )(pm, pool_w, pool_scale)


def _oproj_kernel(pool_ref, y_ref, z_ref, nrm_ref, w_ref, x_ref, g_ref, o_ref, a_ref):
    pw = pool_ref.shape[1]

    @pl.when(pl.program_id(1) == 0)
    def _():
        yz = y_ref[...].astype(F32) * _silu(z_ref[...].astype(F32))
        ms = jnp.mean(yz * yz, axis=-1, keepdims=True)
        a_ref[:, 0:pw] = pool_ref[...]
        a_ref[:, pw:] = (yz * lax.rsqrt(ms + EPS) * nrm_ref[...]).astype(BF16)

    o_ref[...] = x_ref[...] + g_ref[...] * _dot(a_ref[...], w_ref[...])


def _oproj(pooled, y, pm, z_col0, ssd_norm, w_out, x2d, gate, tok_per_batch, tm=512, tn=1024):
    t, d = x2d.shape
    pw = pooled.shape[1]
    sw = y.shape[1]
    assert z_col0 % sw == 0
    zb = z_col0 // sw
    bpb = tok_per_batch // tm
    vmem = 2 * tm * (pw * 2 + sw * 2 + sw * 2) + 2 * (pw + sw) * tn * 2 + tm * (pw + sw) * 2 + 4 * tm * tn * 4 + (8 << 20)
    return pl.pallas_call(
        _oproj_kernel,
        grid=(t // tm, d // tn),
        in_specs=[pl.BlockSpec((tm, pw), lambda i, j: (i, 0)),
                  pl.BlockSpec((tm, sw), lambda i, j: (i, 0)),
                  pl.BlockSpec((tm, sw), lambda i, j: (i, zb)),
                  pl.BlockSpec((1, sw), lambda i, j: (0, 0)),
                  pl.BlockSpec((pw + sw, tn), lambda i, j: (0, j)),
                  pl.BlockSpec((tm, tn), lambda i, j: (i, j)),
                  pl.BlockSpec((None, 1, tn), lambda i, j: (i // bpb, 0, j))],
        out_specs=pl.BlockSpec((tm, tn), lambda i, j: (i, j)),
        out_shape=jax.ShapeDtypeStruct((t, d), F32),
        scratch_shapes=[pltpu.VMEM((tm, pw + sw), BF16)],
        compiler_params=_params(("parallel", "arbitrary"), vmem),
        name="oproj",
    )(pooled, y, pm, ssd_norm.reshape(1, sw), w_out, x2d, gate)


def _extract_top(xs, ridxs, count, sentinel):
    t = xs[0].shape[1]
    slot = lax.broadcasted_iota(jnp.int32, (count, t), 0)

    def body(r, carry):
        out = []
        for (xx, vals, idxs), ridx in zip(carry, ridxs):
            m = jnp.max(xx, axis=0, keepdims=True)
            first = jnp.min(jnp.where(xx == m, ridx, sentinel), axis=0, keepdims=True)
            xx = jnp.where(ridx == first, -jnp.inf, xx)
            vals = jnp.where(slot == r, m, vals)
            idxs = jnp.where(slot == r, first, idxs)
            out.append((xx, vals, idxs))
        return tuple(out)

    init = tuple((x, jnp.zeros((count, t), F32), jnp.zeros((count, t), jnp.int32)) for x in xs)
    res = lax.fori_loop(0, count, body, init)
    return [(vals, idxs) for _, vals, idxs in res]


def _oddeven_pairs(n):
    pairs = []

    def merge(lo, m, r):
        step = r * 2
        if step < m:
            merge(lo, m, step)
            merge(lo + r, m, step)
            for i in range(lo + r, lo + m - r, step):
                pairs.append((i, i + r))
        else:
            pairs.append((lo, lo + r))

    def sort(lo, m):
        if m > 1:
            sort(lo, m // 2)
            sort(lo + m // 2, m // 2)
            merge(lo, m, 1)

    sort(0, n)
    return pairs


def _bitonic_pairs(n):
    pairs = []
    dist = n // 2
    while dist >= 1:
        pairs += [(i, i + dist) for i in range(n) if (i & dist) == 0]
        dist //= 2
    return pairs


_SORT_PAIRS = _oddeven_pairs(PEER_TOPK)
_MERGE_PAIRS = _bitonic_pairs(PEER_TOPK)


def _sorted_top(pieces):
    kk = PEER_TOPK
    v = list(pieces)

    def exchange(i, j):
        v[i], v[j] = jnp.maximum(v[i], v[j]), jnp.minimum(v[i], v[j])

    for i, j in _SORT_PAIRS:
        if j < len(v):
            exchange(i, j)
    v += [jnp.full(v[0].shape, -jnp.inf, F32)] * (kk - len(v))
    for shift in (4, 2, 1):
        other = [pltpu.roll(x, shift, 0) for x in v]
        v = [jnp.maximum(v[j], other[kk - 1 - j]) for j in range(kk)]
        for i, j in _MERGE_PAIRS:
            exchange(i, j)
    return v


def _route_fast(s1, s2):
    nk = PEER_NKEYS
    kk = PEER_TOPK
    t = s1.shape[1]
    p1 = [s1[8 * j:8 * j + 8, :] for j in range(nk // 8)]
    p2 = [s2[8 * j:8 * j + 8, :] for j in range(nk // 8)]
    a = _sorted_top(p1)
    b = _sorted_top(p2)
    sub = lax.broadcasted_iota(jnp.int32, (8, t), 0)

    def pack(vals):
        out = vals[7]
        for s in range(6, -1, -1):
            out = jnp.where(sub == s, vals[s], out)
        return out

    b_lo = pack(b[0:8])
    b_hi = pack(b[8:16])
    a_hi = pack(a[8:16])
    cands = [a[0] + b_lo, a[0] + b_hi] + [a[r] + b_lo for r in range(1, 8)] + [a_hi + b[0]]
    cv = _sorted_top(cands)
    thr = cv[kk - 1]
    zsum = jnp.zeros((8, t), F32)
    for r in range(kk):
        zsum = zsum + jnp.exp(cv[r] - cv[0])

    def count_ge(pieces, level):
        n = jnp.zeros((8, t), F32)
        for p in pieces:
            n = n + jnp.where(p >= level, 1.0, 0.0)
        return jnp.sum(n, axis=0, keepdims=True)

    tied = ((count_ge(cands, thr) != float(kk)) | (count_ge(p1, a[kk - 1]) != float(kk))
            | (count_ge(p2, b[kk - 1]) != float(kk)))
    dup = jnp.zeros((8, t), F32)
    for r in range(kk - 1):
        dup = jnp.maximum(dup, jnp.where((a[r] == a[r + 1]) | (b[r] == b[r + 1]), 1.0, 0.0))
    flag = jnp.max(jnp.maximum(jnp.where(tied, 1.0, 0.0), dup[0:1, :]))
    rank2 = []
    lim1 = []
    for j in range(nk // 8):
        rk = jnp.zeros((8, t), F32)
        lm = jnp.zeros((8, t), F32)
        for r in range(kk):
            rk = rk + jnp.where(b[r] > p2[j], 1.0, 0.0)
            lm = lm + jnp.where(p1[j] + b[r] >= thr, 1.0, 0.0)
        rank2.append(rk)
        lim1.append(jnp.where(p1[j] >= a[kk - 1], lm, 0.0))
    rank2 = jnp.concatenate(rank2, axis=0)
    lim1 = jnp.concatenate(lim1, axis=0)
    e2 = jnp.exp(s2 - b[0][0:1, :])
    g1 = jnp.exp(s1 - a[0][0:1, :]) / zsum[0:1, :]
    return flag, rank2, e2, lim1, g1


def _router_kernel(q_ref, k_ref, r2_ref, e2_ref, l1_ref, g1_ref):
    q = q_ref[...]
    half = q.shape[1] // 2
    s1 = _dot_nt(k_ref[0], q[:, 0:half].astype(BF16))
    s2 = _dot_nt(k_ref[1], q[:, half:].astype(BF16))
    flag, rank2, e2, lim1, g1 = _route_fast(s1, s2)
    r2_ref[...] = rank2.astype(BF16)
    l1_ref[...] = lim1
    e2_ref[...] = e2.astype(BF16)
    g1_ref[...] = g1

    @pl.when(flag > 0.0)
    def _():
        _route_exact(s1, s2, r2_ref, e2_ref, l1_ref, g1_ref)


def _route_exact(s1, s2, r2_ref, e2_ref, l1_ref, g1_ref):
    nk = PEER_NKEYS
    kk = PEER_TOPK
    t = s1.shape[1]
    krow = lax.broadcasted_iota(jnp.int32, (nk, t), 0)
    (a, ia), (b, ib) = _extract_top([s1, s2], [krow, krow], kk, nk)
    cand = jnp.concatenate([a[0:1, :] + b] + [a[r:r + 1, :] + b[0:8, :] for r in range(1, 8)]
                           + [a[8:16, :] + b[0:1, :]], axis=0)
    crow = lax.broadcasted_iota(jnp.int32, (80, t), 0)
    c_r1 = jnp.where(crow < 16, 0, jnp.where(crow < 72, jnp.right_shift(crow - 8, 3), crow - 64))
    c_r2 = jnp.where(crow < 16, crow, jnp.where(crow < 72, crow & 7, 0))
    flat = c_r1 * kk + c_r2
    ((cv, ci),) = _extract_top([cand], [flat], kk, kk * kk)
    zsum = jnp.sum(jnp.exp(cv - cv[0:1, :]), axis=0, keepdims=True)
    slot = lax.broadcasted_iota(jnp.int32, (kk, t), 0)
    r1 = jnp.right_shift(ci, 4)
    cnt = jnp.zeros((kk, t), F32)
    for r in range(kk):
        cnt = cnt + (slot == r1[r:r + 1, :]).astype(F32)
    rank2 = jnp.full((nk, t), float(kk), F32)
    lim1 = jnp.zeros((nk, t), F32)
    for r in range(kk):
        rank2 = jnp.where(krow == ib[r:r + 1, :], float(r), rank2)
        lim1 = jnp.where(krow == ia[r:r + 1, :], cnt[r:r + 1, :], lim1)
    r2_ref[...] = rank2.astype(BF16)
    l1_ref[...] = lim1
    e2_ref[...] = jnp.exp(s2 - b[0:1, :]).astype(BF16)
    g1_ref[...] = jnp.exp(s1 - a[0:1, :]) / zsum


def _router(qry, keys, tm=256):
    t, qw = qry.shape
    hq = qw // PEER_HEADS
    nk = PEER_NKEYS
    spec_o = pl.BlockSpec((None, nk, tm), lambda i, h: (h, 0, i))
    shp_b = jax.ShapeDtypeStruct((PEER_HEADS, nk, t), BF16)
    shp_f = jax.ShapeDtypeStruct((PEER_HEADS, nk, t), F32)
    return pl.pallas_call(
        _router_kernel,
        grid=(t // tm, PEER_HEADS),
        in_specs=[pl.BlockSpec((tm, hq), lambda i, h: (i, h)),
                  pl.BlockSpec((None, 2, nk, hq // 2), lambda i, h: (h, 0, 0, 0))],
        out_specs=[spec_o, spec_o, spec_o, spec_o],
        out_shape=[shp_b, shp_b, shp_f, shp_f],
        compiler_params=_params(("parallel", "arbitrary"), 32 << 20),
        name="router",
    )(qry, keys)


def _gelu(x):
    return 0.5 * x * (1.0 + lax.erf(x * (2.0 ** -0.5)))


def _ffn_kernel(ht_ref, u_ref, vt_ref, r2_ref, e2_ref, l1_ref, g1_ref, o_ref, acc_ref, *, te, tc, rc):
    e = pl.program_id(1)
    nk = PEER_NKEYS
    sub = BF16_SUBLANES
    d, tm = ht_ref.shape
    ns = te // nk
    nt = tm // tc

    @pl.when(e == 0)
    def _():
        acc_ref[...] = jnp.zeros_like(acc_ref)

    def scores(ti):
        return _dot(u_ref[...], ht_ref[:, ti * tc:(ti + 1) * tc])

    def gates(ti):
        c0 = ti * tc
        tiles = []
        for s in range(ns):
            k1 = e * ns + s
            lims = []
            gscs = []
            for hd in range(PEER_HEADS):
                lims.append(jnp.broadcast_to(l1_ref[hd, pl.ds(k1, 1), c0:c0 + tc], (sub, tc)).astype(BF16))
                gscs.append(jnp.broadcast_to(g1_ref[hd, pl.ds(k1, 1), c0:c0 + tc], (sub, tc)).astype(BF16))
            for j in range(nk // sub):
                w = jnp.zeros((sub, tc), BF16)
                for hd in range(PEER_HEADS):
                    r2 = r2_ref[hd, j * sub:(j + 1) * sub, c0:c0 + tc]
                    e2 = e2_ref[hd, j * sub:(j + 1) * sub, c0:c0 + tc]
                    w = w + jnp.where(r2 < lims[hd], e2, jnp.zeros_like(e2)) * gscs[hd]
                tiles.append(w)
        return jnp.concatenate(tiles, axis=0)

    def accumulate(ti, g):
        c0 = ti * tc
        for r0 in range(0, d, rc):
            acc_ref[r0:r0 + rc, c0:c0 + tc] += _dot(vt_ref[r0:r0 + rc, :], g)

    at = scores(0)
    w = gates(0)
    for ti in range(nt):
        at_next = scores(ti + 1) if ti + 1 < nt else None
        g = (_gelu(at) * w.astype(F32)).astype(BF16)
        w_next = gates(ti + 1) if ti + 1 < nt else None
        accumulate(ti, g)
        at, w = at_next, w_next

    @pl.when(e == pl.num_programs(1) - 1)
    def _():
        o_ref[...] = acc_ref[...].astype(o_ref.dtype)


def _ffn(ht, u, vt, r2, e2, l1, g1, tm=512, te=512, tc=256, rc=1024):
    d, t = ht.shape
    ne = u.shape[0]
    nk = PEER_NKEYS
    once = pl.Buffered(1)
    rspec = pl.BlockSpec((PEER_HEADS, nk, tm), lambda i, e: (0, 0, i), pipeline_mode=once)
    vmem = (d * tm * 2 + 2 * PEER_HEADS * nk * tm * (2 + 4) + 2 * 2 * te * d * 2 + d * tm * 4 + 2 * d * tm * 2
            + 2 * te * tc * 4 + rc * tc * 4 + (6 << 20))
    return pl.pallas_call(
        functools.partial(_ffn_kernel, te=te, tc=tc, rc=rc),
        grid=(t // tm, ne // te),
        in_specs=[pl.BlockSpec((d, tm), lambda i, e: (0, i), pipeline_mode=once),
                  pl.BlockSpec((te, d), lambda i, e: (e, 0)),
                  pl.BlockSpec((d, te), lambda i, e: (0, e)),
                  rspec, rspec, rspec, rspec],
        out_specs=pl.BlockSpec((d, tm), lambda i, e: (0, i)),
        out_shape=jax.ShapeDtypeStruct((d, t), BF16),
        scratch_shapes=[pltpu.VMEM((d, tm), F32)],
        compiler_params=_params(("parallel", "arbitrary"), vmem),
        name="peer_ffn",
    )(ht, u, vt, r2, e2, l1, g1)


def _final_kernel(x_ref, ft_ref, g_ref, n_ref, o_ref):
    x = x_ref[...] + g_ref[...] * ft_ref[...].astype(F32).T
    ms = jnp.mean(x * x, axis=-1, keepdims=True)
    o_ref[...] = x * lax.rsqrt(ms + EPS) * n_ref[...]


def _final(x2d, ffn_t, gate, final_norm, tok_per_batch, tm=256):
    t, d = x2d.shape
    bpb = tok_per_batch // tm
    return pl.pallas_call(
        _final_kernel,
        grid=(t // tm,),
        in_specs=[pl.BlockSpec((tm, d), lambda i: (i, 0)),
                  pl.BlockSpec((d, tm), lambda i: (0, i)),
                  pl.BlockSpec((None, 1, d), lambda i: (i // bpb, 0, 0)),
                  pl.BlockSpec((1, d), lambda i: (0, 0))],
        out_specs=pl.BlockSpec((tm, d), lambda i: (i, 0)),
        out_shape=jax.ShapeDtypeStruct((t, d), F32),
        compiler_params=_params(("parallel",), 6 * tm * d * 4 + (8 << 20)),
        name="final",
    )(x2d, ffn_t, gate, final_norm.reshape(1, d))


def kernel(x, c, ctx, c_ctx, w_mod, b_mod, norm1, norm2, w_in, pool_w, pool_scale, conv_w, conv_b, dt_bias, a_log,
           d_skip, ssd_norm, w_out, peer_wq, peer_keys, peer_u, peer_v, final_norm):
    assert w_mod.shape[0] == 1, "single-layer trunk"
    bsz, n, d = x.shape
    n_ctx = ctx.shape[1]
    g = SSD_GROUPS
    inner = g * SSD_REP * SSD_HEAD_DIM
    gn = g * SSD_STATE
    pool_wd = POOL_GROUPS * pool_w.shape[2]
    off_z = pool_wd
    off_xbc = off_z + inner
    off_dt = off_xbc + inner + 2 * gn
    heads = g * SSD_REP

    cin = jnp.concatenate([c, c_ctx[None, :], jnp.zeros((8 - bsz - 1, d), F32)], axis=0)
    mod = _mod(cin, w_mod[0], b_mod[0])
    lat = mod[:bsz].reshape(bsz, 6, 1, d)
    sh1, sc1, g1, sh2, sc2, g2 = (lat[:, k] for k in range(6))
    cmod = mod[bsz].reshape(6, 1, 1, d)
    csh1, csc1 = cmod[0], cmod[1]

    w_in_b = w_in[0].astype(BF16)
    w_dt = jnp.pad(w_in_b[:, off_dt:], ((0, 0), (0, LANES - 2 * heads)))

    conv_wb = jnp.concatenate([conv_w[0], conv_b[0][None, :],
                               jnp.zeros((8 - SSD_CONV - 1, conv_w.shape[2]), F32)], axis=0)
    prm = jnp.stack([dt_bias[0].reshape(2, g, SSD_REP).transpose(1, 0, 2).reshape(g, 8),
                     a_log[0].reshape(2, g, SSD_REP).transpose(1, 0, 2).reshape(g, 8)], axis=-1)
    dsk = jnp.repeat(d_skip[0], SSD_HEAD_DIM).reshape(g, 1, SSD_REP * SSD_HEAD_DIM)

    def dt_rows(dt_out, nb, nt):
        r = dt_out[:, :2 * heads].reshape(nb, nt, 2, g, SSD_REP)
        return r.transpose(0, 3, 2, 4, 1).reshape(nb, g, 8, nt)

    pc, dtc = _nmm(ctx.reshape(bsz * n_ctx, d), norm1[0], csc1, csh1, w_in_b, col0=off_xbc, ncols=off_dt - off_xbc,
                   tok_per_batch=n_ctx, out_dtype=BF16, w_dt=w_dt, tm=256, tn=1024, name="inproj_ctx")
    zeros_h = jnp.zeros((bsz, g, SSD_STATE, 2 * SSD_REP * SSD_HEAD_DIM), F32)
    h_ctx = _ssd(pc.reshape(bsz, n_ctx, -1), 0, dt_rows(dtc, bsz, n_ctx), conv_wb, prm, dsk, zeros_h,
                 with_output=False)

    x2d = x.reshape(bsz * n, d)
    pm, dtl = _nmm(x2d, norm1[0], sc1, sh1, w_in_b, col0=0, ncols=off_dt, tok_per_batch=n, out_dtype=BF16,
                   w_dt=w_dt, tn=1024, name="inproj")
    pm3 = pm.reshape(bsz, n, off_dt)
    y = _ssd(pm3, off_xbc, dt_rows(dtl, bsz, n), conv_wb, prm, dsk, h_ctx, with_output=True)
    pooled = _pool(pm3, pool_w[0].astype(BF16), pool_scale[0].reshape(POOL_GROUPS, 1, -1))
    x1 = _oproj(pooled.reshape(bsz * n, pool_wd), y.reshape(bsz * n, inner), pm, off_z, ssd_norm[0],
                w_out[0].astype(BF16), x2d, g1, n)

    qry, h2t = _nmm(x1, norm2[0], sc2, sh2, peer_wq[0].astype(BF16), col0=0, ncols=peer_wq.shape[2],
                    tok_per_batch=n, out_dtype=F32, emit_ht=True, name="peer_q")
    r2, e2, l1, gg = _router(qry, peer_keys[0].astype(BF16))
    ffn_t = _ffn(h2t, peer_u[0].astype(BF16), peer_v[0].astype(BF16).T, r2, e2, l1, gg)
    out = _final(x1, ffn_t, g2, final_norm, n)
    return out.reshape(bsz, n, d)
```

```python
import functools

import jax
import jax.numpy as jnp
from jax import lax
from jax.experimental import pallas as pl
from jax.experimental.pallas import tpu as pltpu

F32 = jnp.float32
BF16 = jnp.bfloat16
HIGHEST = lax.Precision.HIGHEST

EPS = 1e-6
GRID_W = 64
POOL_GROUPS = 4
SSD_HEAD_DIM = 64
SSD_GROUPS = 8
SSD_REP = 4
SSD_STATE = 128
SSD_CONV = 5
SSD_CHUNK = 128
PEER_HEADS = 8
PEER_NKEYS = 128
PEER_TOPK = 16

LANES = 128
BF16_SUBLANES = 16
VMEM_CAP = 62 * 1024 * 1024


def _params(sem, vmem_bytes):
    return pltpu.CompilerParams(dimension_semantics=sem,
                                vmem_limit_bytes=int(min(max(vmem_bytes, 16 << 20), VMEM_CAP)))


def _dot(a, b):
    return jnp.dot(a, b, preferred_element_type=F32)


def _dot_nt(a, b):
    return lax.dot_general(a, b, (((1,), (1,)), ((), ())), preferred_element_type=F32)


def _silu(x):
    return x * jax.nn.sigmoid(x)


def _mod_kernel(c_ref, w_ref, b_ref, o_ref):
    s = _silu(c_ref[...]).astype(BF16)
    o_ref[...] = _dot(s, w_ref[...].astype(BF16)) + b_ref[...]


def _mod(cin, w, b):
    rows, d = cin.shape
    n = w.shape[1]
    tn = 512
    return pl.pallas_call(
        _mod_kernel,
        grid=(n // tn,),
        in_specs=[pl.BlockSpec((rows, d), lambda j: (0, 0)),
                  pl.BlockSpec((d, tn), lambda j: (0, j)),
                  pl.BlockSpec((1, tn), lambda j: (0, j))],
        out_specs=pl.BlockSpec((rows, tn), lambda j: (0, j)),
        out_shape=jax.ShapeDtypeStruct((rows, n), F32),
        compiler_params=_params(("arbitrary",), 2 * d * tn * 4 + (4 << 20)),
        name="mod",
    )(cin, w, b.reshape(1, n))


def _nmm_kernel(*refs, has_dt, emit_ht):
    x_ref, g_ref, sc_ref, sh_ref, w_ref = refs[:5]
    k = 5
    wdt_ref = None
    if has_dt:
        wdt_ref = refs[k]
        k += 1
    o_ref = refs[k]
    k += 1
    dt_ref = None
    if has_dt:
        dt_ref = refs[k]
        k += 1
    ht_ref = None
    if emit_ht:
        ht_ref = refs[k]
        k += 1
    h_ref = refs[k]

    @pl.when(pl.program_id(1) == 0)
    def _():
        x = x_ref[...]
        ms = jnp.mean(x * x, axis=-1, keepdims=True)
        y = x * lax.rsqrt(ms + EPS) * g_ref[...]
        h32 = y * (1.0 + sc_ref[...]) + sh_ref[...]
        h = h32.astype(BF16)
        h_ref[...] = h
        if has_dt:
            dt_ref[...] = _dot(h, wdt_ref[...])
        if emit_ht:
            for r0 in range(0, h32.shape[0], LANES):
                ht_ref[:, r0:r0 + LANES] = h32[r0:r0 + LANES, :].T.astype(BF16)

    o_ref[...] = _dot(h_ref[...], w_ref[...]).astype(o_ref.dtype)


def _nmm(x2d, gain, sc, sh, w, *, col0, ncols, tok_per_batch, out_dtype, w_dt=None, emit_ht=False,
         tm=512, tn=512, name="nmm"):
    t, d = x2d.shape
    nb = sc.shape[0]
    assert t % tm == 0 and ncols % tn == 0 and col0 % tn == 0 and tok_per_batch % tm == 0
    jb0 = col0 // tn
    if nb == 1:
        bmap = lambda i, j: (0, 0, 0)
    else:
        bpb = tok_per_batch // tm
        bmap = lambda i, j: (i // bpb, 0, 0)
    in_specs = [pl.BlockSpec((tm, d), lambda i, j: (i, 0)),
                pl.BlockSpec((1, d), lambda i, j: (0, 0)),
                pl.BlockSpec((None, 1, d), bmap),
                pl.BlockSpec((None, 1, d), bmap),
                pl.BlockSpec((d, tn), lambda i, j: (0, j + jb0))]
    args = [x2d, gain.reshape(1, d), sc, sh, w]
    out_specs = [pl.BlockSpec((tm, tn), lambda i, j: (i, j))]
    out_shape = [jax.ShapeDtypeStruct((t, ncols), out_dtype)]
    has_dt = w_dt is not None
    if has_dt:
        in_specs.append(pl.BlockSpec((d, LANES), lambda i, j: (0, 0)))
        args.append(w_dt)
        out_specs.append(pl.BlockSpec((tm, LANES), lambda i, j: (i, 0)))
        out_shape.append(jax.ShapeDtypeStruct((t, LANES), F32))
    if emit_ht:
        out_specs.append(pl.BlockSpec((d, tm), lambda i, j: (0, i)))
        out_shape.append(jax.ShapeDtypeStruct((d, t), BF16))
    vmem = (2 * tm * d * 4 + 2 * d * tn * 2 + tm * d * 2 + 2 * tm * tn * 4 + (2 * tm * d * 2 if emit_ht else 0)
            + (10 << 20))
    return pl.pallas_call(
        functools.partial(_nmm_kernel, has_dt=has_dt, emit_ht=emit_ht),
        grid=(t // tm, ncols // tn),
        in_specs=in_specs,
        out_specs=out_specs,
        out_shape=out_shape,
        scratch_shapes=[pltpu.VMEM((tm, d), BF16)],
        compiler_params=_params(("parallel", "arbitrary"), vmem),
        name=name,
    )(*args)


def _softplus(x):
    return jnp.maximum(x, 0.0) + jnp.log1p(jnp.exp(-jnp.abs(x)))


def _expand_heads(cols, first):
    rows = cols.shape[0]
    lane = lax.broadcasted_iota(jnp.int32, (rows, LANES), 1)
    parts = []
    for q in range(2):
        lo = cols[:, first + 2 * q:first + 2 * q + 1]
        hi = cols[:, first + 2 * q + 1:first + 2 * q + 2]
        parts.append(jnp.where(lane < SSD_HEAD_DIM, lo, hi))
    return jnp.concatenate(parts, axis=1)


def _ssd_kernel(xh_ref, bm_ref, cm_ref, cwx_ref, cwb_ref, cwc_ref, dtr_ref, prm_ref, dsk_ref, h0_ref,
                *rest, n_tok, with_output):
    if with_output:
        y_ref, xc_ref, ext_ref, st_ref, dec_ref, rows_ref, cols_ref = rest
        hout_ref = None
    else:
        hout_ref, xc_ref, ext_ref, st_ref, dec_ref, rows_ref, cols_ref = rest
        y_ref = None
    q = SSD_CHUNK
    nc = n_tok // q
    hw = SSD_REP * SSD_HEAD_DIM
    halo = 16

    def conv_body(c, carry):
        r0 = pl.multiple_of(c * q, q)
        rp = pl.multiple_of(jnp.maximum(c * q - halo, 0), halo)
        rn = pl.multiple_of(jnp.minimum(c * q + q, n_tok - halo), halo)
        keep_p = jnp.where(c > 0, 1.0, 0.0)
        keep_n = jnp.where(c < nc - 1, 1.0, 0.0)
        for ref, cw_ref, l0, wd in ((xh_ref, cwx_ref, 0, hw), (bm_ref, cwb_ref, hw, SSD_STATE),
                                    (cm_ref, cwc_ref, hw + SSD_STATE, SSD_STATE)):
            ext_ref[0:halo, l0:l0 + wd] = ref[pl.ds(rp, halo), :].astype(F32) * keep_p
            ext_ref[halo:halo + q, l0:l0 + wd] = ref[pl.ds(r0, q), :].astype(F32)
            ext_ref[halo + q:2 * halo + q, l0:l0 + wd] = ref[pl.ds(rn, halo), :].astype(F32) * keep_n
            cw = cw_ref[...]
            acc = jnp.zeros((q, wd), F32) + cw[SSD_CONV:SSD_CONV + 1, :]
            for k in range(SSD_CONV):
                off = halo - SSD_CONV // 2 + k
                acc = acc + cw[k:k + 1, :] * ext_ref[off:off + q, l0:l0 + wd]
            xc_ref[pl.ds(r0, q), l0:l0 + wd] = _silu(acc).astype(BF16)
        return carry

    lax.fori_loop(0, nc, conv_body, 0)

    prm = prm_ref[...]
    bias_r = prm[:, 0:1]
    a_r = -jnp.exp(prm[:, 1:2])
    dt_all = _softplus(dtr_ref[...] + bias_r)
    x_all = dt_all * a_r
    for c in range(nc):
        rows_ref[0, 8 * c:8 * c + 8, :] = x_all[:, c * q:(c + 1) * q]
        rows_ref[1, 8 * c:8 * c + 8, :] = dt_all[:, c * q:(c + 1) * q]
    xs = rows_ref[0]
    dts = rows_ref[1]
    jj = lax.broadcasted_iota(jnp.int32, (q, q), 0)
    ii = lax.broadcasted_iota(jnp.int32, (q, q), 1)
    tri_f = (jj <= ii).astype(F32)
    tri_b = (jj >= ii).astype(F32)
    cs_f = jnp.dot(xs, tri_f, precision=HIGHEST, preferred_element_type=F32)
    cs_b = jnp.dot(xs, tri_b, precision=HIGHEST, preferred_element_type=F32)
    rowk = lax.broadcasted_iota(jnp.int32, (8 * nc, q), 0) & 7
    cs = jnp.where(rowk < SSD_REP, cs_f, cs_b)
    total = jnp.sum(xs, axis=1, keepdims=True)
    wgt = jnp.exp(total - cs) * dts
    ecs = jnp.exp(cs)
    rows_ref[0] = cs
    lane_k = jnp.right_shift(lax.broadcasted_iota(jnp.int32, (8 * nc, 2 * hw), 1), 6)
    row_k = lax.broadcasted_iota(jnp.int32, (8 * nc, 2 * hw), 0) & 7
    dec = jnp.where(lane_k == row_k, jnp.exp(total), 0.0)
    pad = jnp.zeros((q - 24, q), F32)
    for c in range(nc):
        sl = slice(8 * c, 8 * c + 8)
        dec_ref[c] = dec[sl]
        cols_ref[c] = jnp.concatenate([cs[sl], wgt[sl], ecs[sl], pad], axis=0).T

    def state_body(c, carry):
        r0 = pl.multiple_of(c * q, q)
        cols = cols_ref[c]
        xc = xc_ref[pl.ds(r0, q), :]
        xh = xc[:, 0:hw].astype(F32)
        xw = jnp.concatenate([xh * _expand_heads(cols, 8), xh * _expand_heads(cols, 8 + SSD_REP)],
                             axis=1).astype(BF16)
        bm_t = xc[:, hw:hw + SSD_STATE].astype(F32).T.astype(BF16)
        st_ref[c] = _dot(bm_t, xw)
        return carry

    lax.fori_loop(0, nc, state_body, 0, unroll=2)

    def rec_body(i, carry):
        hf, hb = carry
        cf = i
        cb = nc - 1 - i
        sf = st_ref[cf, :, 0:hw]
        st_ref[cf, :, 0:hw] = hf
        hf = hf * jnp.sum(dec_ref[cf, :, 0:hw], axis=0, keepdims=True) + sf
        sb = st_ref[cb, :, hw:2 * hw]
        st_ref[cb, :, hw:2 * hw] = hb
        hb = hb * jnp.sum(dec_ref[cb, :, hw:2 * hw], axis=0, keepdims=True) + sb
        return hf, hb

    h0 = h0_ref[...]
    hf, hb = lax.fori_loop(0, nc, rec_body, (h0[:, 0:hw], h0[:, hw:2 * hw]))

    if not with_output:
        hout_ref[:, 0:hw] = hf
        hout_ref[:, hw:2 * hw] = hb
        return

    lane_head = jnp.right_shift(lax.broadcasted_iota(jnp.int32, (q, hw), 1), 6)
    dsk = dsk_ref[...]

    def out_body(c, carry):
        r0 = pl.multiple_of(c * q, q)
        r8 = pl.multiple_of(c * 8, 8)
        cols = cols_ref[c]
        cs_r = rows_ref[0, pl.ds(r8, 8), :]
        dt_r = rows_ref[1, pl.ds(r8, 8), :]
        xc = xc_ref[pl.ds(r0, q), :]
        xh_b = xc[:, 0:hw]
        bm = xc[:, hw:hw + SSD_STATE]
        cm = xc[:, hw + SSD_STATE:hw + 2 * SSD_STATE]
        cbm = _dot_nt(cm, bm)
        ydiag = jnp.zeros((q, hw), F32)
        for r in range(SSD_REP):
            seg_f = cols[:, r:r + 1] - cs_r[r:r + 1, :]
            l_f = jnp.exp(jnp.where(ii <= jj, seg_f, -jnp.inf)) * dt_r[r:r + 1, :]
            rb = SSD_REP + r
            seg_b = cols[:, rb:rb + 1] - cs_r[rb:rb + 1, :]
            l_b = jnp.exp(jnp.where(ii >= jj, seg_b, -jnp.inf)) * dt_r[rb:rb + 1, :]
            m = (cbm * (l_f + l_b)).astype(BF16)
            ydiag = ydiag + jnp.where(lane_head == r, _dot(m, xh_b), 0.0)
        st = st_ref[c]
        yo_f = _dot(cm, st[:, 0:hw].astype(BF16)) * _expand_heads(cols, 16)
        yo_b = _dot(cm, st[:, hw:2 * hw].astype(BF16)) * _expand_heads(cols, 16 + SSD_REP)
        y_ref[pl.ds(r0, q), :] = (ydiag + yo_f + yo_b + dsk * xh_b.astype(F32)).astype(y_ref.dtype)
        return carry

    lax.fori_loop(0, nc, out_body, 0, unroll=2)


def _ssd(xbc, col0, dtr, conv_wb, prm, dsk, h0, *, with_output):
    b, n, _ = xbc.shape
    g = SSD_GROUPS
    hw = SSD_REP * SSD_HEAD_DIM
    inner = g * hw
    assert col0 % hw == 0 and n % SSD_CHUNK == 0
    xb0 = col0 // hw
    bb0 = (col0 + inner) // SSD_STATE
    cb0 = (col0 + inner + g * SSD_STATE) // SSD_STATE
    nc = n // SSD_CHUNK
    in_specs = [
        pl.BlockSpec((None, n, hw), lambda bi, gi: (bi, 0, xb0 + gi)),
        pl.BlockSpec((None, n, SSD_STATE), lambda bi, gi: (bi, 0, bb0 + gi)),
        pl.BlockSpec((None, n, SSD_STATE), lambda bi, gi: (bi, 0, cb0 + gi)),
        pl.BlockSpec((8, hw), lambda bi, gi: (0, gi)),
        pl.BlockSpec((8, SSD_STATE), lambda bi, gi: (0, inner // SSD_STATE + gi)),
        pl.BlockSpec((8, SSD_STATE), lambda bi, gi: (0, (inner + g * SSD_STATE) // SSD_STATE + gi)),
        pl.BlockSpec((None, None, 8, n), lambda bi, gi: (bi, gi, 0, 0)),
        pl.BlockSpec((None, 8, 2), lambda bi, gi: (gi, 0, 0)),
        pl.BlockSpec((None, 1, hw), lambda bi, gi: (gi, 0, 0)),
        pl.BlockSpec((None, None, SSD_STATE, 2 * hw), lambda bi, gi: (bi, gi, 0, 0)),
    ]
    if with_output:
        out_specs = pl.BlockSpec((None, n, hw), lambda bi, gi: (bi, 0, gi))
        out_shape = jax.ShapeDtypeStruct((b, n, inner), BF16)
    else:
        out_specs = pl.BlockSpec((None, None, SSD_STATE, 2 * hw), lambda bi, gi: (bi, gi, 0, 0))
        out_shape = jax.ShapeDtypeStruct((b, g, SSD_STATE, 2 * hw), F32)
    vmem = (2 * n * 2 * hw * 2 + n * 2 * hw * 2 + nc * SSD_STATE * 2 * hw * 4 + 2 * n * hw * 4
            + 2 * 8 * n * 4 + nc * SSD_CHUNK * SSD_CHUNK * 4 + (8 << 20))
    return pl.pallas_call(
        functools.partial(_ssd_kernel, n_tok=n, with_output=with_output),
        grid=(b, g),
        in_specs=in_specs,
        out_specs=out_specs,
        out_shape=out_shape,
        scratch_shapes=[pltpu.VMEM((n, 2 * hw), BF16),
                        pltpu.VMEM((SSD_CHUNK + 32, 2 * hw), F32),
                        pltpu.VMEM((nc, SSD_STATE, 2 * hw), F32),
                        pltpu.VMEM((nc, 8, 2 * hw), F32),
                        pltpu.VMEM((2, 8 * nc, SSD_CHUNK), F32),
                        pltpu.VMEM((nc, SSD_CHUNK, SSD_CHUNK), F32)],
        compiler_params=_params(("parallel", "arbitrary"), vmem),
        name="ssd_out" if with_output else "ssd_state",
    )(xbc, xbc, xbc, conv_wb, conv_wb, conv_wb, dtr, prm, dsk, h0)


def _pool_kernel(v_ref, w_ref, sc_ref, o_ref, rm_ref, *, rows):
    k = pl.program_id(1)
    win = jnp.left_shift(2, k)
    half = win // 2
    gc = v_ref.shape[1]

    def row_body(r, carry):
        lo = jnp.maximum(r - half, 0)
        hi = jnp.minimum(r - half + win, rows)

        def acc_body(rr, acc):
            return acc + v_ref[pl.ds(pl.multiple_of(rr * GRID_W, GRID_W), GRID_W), :].astype(F32)

        tot = lax.fori_loop(lo, hi, acc_body, jnp.zeros((GRID_W, gc), F32))
        cnt = (hi - lo).astype(F32)
        rm_ref[pl.ds(pl.multiple_of(r * GRID_W, GRID_W), GRID_W), :] = (tot / cnt).astype(BF16)
        return carry

    lax.fori_loop(0, rows, row_body, 0)

    blk = min(8, rows) * GRID_W
    ti = lax.broadcasted_iota(jnp.int32, (blk, blk), 0)
    tj = lax.broadcasted_iota(jnp.int32, (blk, blk), 1)
    ci = ti & (GRID_W - 1)
    cj = tj & (GRID_W - 1)
    same_row = jnp.right_shift(ti, 6) == jnp.right_shift(tj, 6)
    band = jnp.where(same_row & (cj >= ci - half) & (cj < ci - half + win), 1.0, 0.0).astype(BF16)
    cc = lax.broadcasted_iota(jnp.int32, (blk, gc), 0) & (GRID_W - 1)
    cnt_c = (jnp.minimum(cc - half + win, GRID_W) - jnp.maximum(cc - half, 0)).astype(F32)
    w = w_ref[...]
    scale = sc_ref[...]

    def col_body(t, carry):
        r0 = pl.multiple_of(t * blk, blk)
        m = _dot(band, rm_ref[pl.ds(r0, blk), :]) / cnt_c
        p = (m - v_ref[pl.ds(r0, blk), :].astype(F32)).astype(BF16)
        o_ref[pl.ds(r0, blk), :] = (_dot(p, w) * scale).astype(o_ref.dtype)
        return carry

    lax.fori_loop(0, (rows * GRID_W) // blk, col_body, 0)


def _pool(pm, pool_w, pool_scale):
    b, n, _ = pm.shape
    gc = pool_w.shape[1]
    rows = n // GRID_W
    vmem = 2 * n * gc * 2 + n * gc * 2 + 2 * n * gc * 2 + 2 * gc * gc * 2 + (8 << 20)
    return pl.pallas_call(
        functools.partial(_pool_kernel, rows=rows),
        grid=(b, POOL_GROUPS),
        in_specs=[pl.BlockSpec((None, n, gc), lambda bi, ki: (bi, 0, ki)),
                  pl.BlockSpec((None, gc, gc), lambda bi, ki: (ki, 0, 0)),
                  pl.BlockSpec((None, 1, gc), lambda bi, ki: (ki, 0, 0))],
        out_specs=pl.BlockSpec((None, n, gc), lambda bi, ki: (bi, 0, ki)),
        out_shape=jax.ShapeDtypeStruct((b, n, POOL_GROUPS * gc), BF16),
        scratch_shapes=[pltpu.VMEM((n, gc), BF16)],
        compiler_params=_params(("parallel", "arbitrary"), vmem),
        name="pool",
    )(pm, pool_w, pool_scale)


def _oproj_kernel(pool_ref, y_ref, z_ref, nrm_ref, w_ref, x_ref, g_ref, o_ref, a_ref):
    pw = pool_ref.shape[1]

    @pl.when(pl.program_id(1) == 0)
    def _():
        yz = y_ref[...].astype(F32) * _silu(z_ref[...].astype(F32))
        ms = jnp.mean(yz * yz, axis=-1, keepdims=True)
        a_ref[:, 0:pw] = pool_ref[...]
        a_ref[:, pw:] = (yz * lax.rsqrt(ms + EPS) * nrm_ref[...]).astype(BF16)

    o_ref[...] = x_ref[...] + g_ref[...] * _dot(a_ref[...], w_ref[...])


def _oproj(pooled, y, pm, z_col0, ssd_norm, w_out, x2d, gate, tok_per_batch, tm=512, tn=1024):
    t, d = x2d.shape
    pw = pooled.shape[1]
    sw = y.shape[1]
    assert z_col0 % sw == 0
    zb = z_col0 // sw
    bpb = tok_per_batch // tm
    vmem = 2 * tm * (pw * 2 + sw * 2 + sw * 2) + 2 * (pw + sw) * tn * 2 + tm * (pw + sw) * 2 + 4 * tm * tn * 4 + (8 << 20)
    return pl.pallas_call(
        _oproj_kernel,
        grid=(t // tm, d // tn),
        in_specs=[pl.BlockSpec((tm, pw), lambda i, j: (i, 0)),
                  pl.BlockSpec((tm, sw), lambda i, j: (i, 0)),
                  pl.BlockSpec((tm, sw), lambda i, j: (i, zb)),
                  pl.BlockSpec((1, sw), lambda i, j: (0, 0)),
                  pl.BlockSpec((pw + sw, tn), lambda i, j: (0, j)),
                  pl.BlockSpec((tm, tn), lambda i, j: (i, j)),
                  pl.BlockSpec((None, 1, tn), lambda i, j: (i // bpb, 0, j))],
        out_specs=pl.BlockSpec((tm, tn), lambda i, j: (i, j)),
        out_shape=jax.ShapeDtypeStruct((t, d), F32),
        scratch_shapes=[pltpu.VMEM((tm, pw + sw), BF16)],
        compiler_params=_params(("parallel", "arbitrary"), vmem),
        name="oproj",
    )(pooled, y, pm, ssd_norm.reshape(1, sw), w_out, x2d, gate)


def _extract_top(xs, ridxs, count, sentinel):
    t = xs[0].shape[1]
    slot = lax.broadcasted_iota(jnp.int32, (count, t), 0)

    def body(r, carry):
        out = []
        for (xx, vals, idxs), ridx in zip(carry, ridxs):
            m = jnp.max(xx, axis=0, keepdims=True)
            first = jnp.min(jnp.where(xx == m, ridx, sentinel), axis=0, keepdims=True)
            xx = jnp.where(ridx == first, -jnp.inf, xx)
            vals = jnp.where(slot == r, m, vals)
            idxs = jnp.where(slot == r, first, idxs)
            out.append((xx, vals, idxs))
        return tuple(out)

    init = tuple((x, jnp.zeros((count, t), F32), jnp.zeros((count, t), jnp.int32)) for x in xs)
    res = lax.fori_loop(0, count, body, init)
    return [(vals, idxs) for _, vals, idxs in res]


def _oddeven_pairs(n):
    pairs = []

    def merge(lo, m, r):
        step = r * 2
        if step < m:
            merge(lo, m, step)
            merge(lo + r, m, step)
            for i in range(lo + r, lo + m - r, step):
                pairs.append((i, i + r))
        else:
            pairs.append((lo, lo + r))

    def sort(lo, m):
        if m > 1:
            sort(lo, m // 2)
            sort(lo + m // 2, m // 2)
            merge(lo, m, 1)

    sort(0, n)
    return pairs


def _bitonic_pairs(n):
    pairs = []
    dist = n // 2
    while dist >= 1:
        pairs += [(i, i + dist) for i in range(n) if (i & dist) == 0]
        dist //= 2
    return pairs


_SORT_PAIRS = _oddeven_pairs(PEER_TOPK)
_MERGE_PAIRS = _bitonic_pairs(PEER_TOPK)


def _sorted_top(pieces):
    kk = PEER_TOPK
    v = list(pieces)

    def exchange(i, j):
        v[i], v[j] = jnp.maximum(v[i], v[j]), jnp.minimum(v[i], v[j])

    for i, j in _SORT_PAIRS:
        if j < len(v):
            exchange(i, j)
    v += [jnp.full(v[0].shape, -jnp.inf, F32)] * (kk - len(v))
    for shift in (4, 2, 1):
        other = [pltpu.roll(x, shift, 0) for x in v]
        v = [jnp.maximum(v[j], other[kk - 1 - j]) for j in range(kk)]
        for i, j in _MERGE_PAIRS:
            exchange(i, j)
    return v


def _route_fast(s1, s2):
    nk = PEER_NKEYS
    kk = PEER_TOPK
    t = s1.shape[1]
    p1 = [s1[8 * j:8 * j + 8, :] for j in range(nk // 8)]
    p2 = [s2[8 * j:8 * j + 8, :] for j in range(nk // 8)]
    a = _sorted_top(p1)
    b = _sorted_top(p2)
    sub = lax.broadcasted_iota(jnp.int32, (8, t), 0)

    def pack(vals):
        out = vals[7]
        for s in range(6, -1, -1):
            out = jnp.where(sub == s, vals[s], out)
        return out

    b_lo = pack(b[0:8])
    b_hi = pack(b[8:16])
    a_hi = pack(a[8:16])
    cands = [a[0] + b_lo, a[0] + b_hi] + [a[r] + b_lo for r in range(1, 8)] + [a_hi + b[0]]
    cv = _sorted_top(cands)
    thr = cv[kk - 1]
    zsum = jnp.zeros((8, t), F32)
    for r in range(kk):
        zsum = zsum + jnp.exp(cv[r] - cv[0])

    def count_ge(pieces, level):
        n = jnp.zeros((8, t), F32)
        for p in pieces:
            n = n + jnp.where(p >= level, 1.0, 0.0)
        return jnp.sum(n, axis=0, keepdims=True)

    tied = ((count_ge(cands, thr) != float(kk)) | (count_ge(p1, a[kk - 1]) != float(kk))
            | (count_ge(p2, b[kk - 1]) != float(kk)))
    dup = jnp.zeros((8, t), F32)
    for r in range(kk - 1):
        dup = jnp.maximum(dup, jnp.where((a[r] == a[r + 1]) | (b[r] == b[r + 1]), 1.0, 0.0))
    flag = jnp.max(jnp.maximum(jnp.where(tied, 1.0, 0.0), dup[0:1, :]))
    rank2 = []
    lim1 = []
    for j in range(nk // 8):
        rk = jnp.zeros((8, t), F32)
        lm = jnp.zeros((8, t), F32)
        for r in range(kk):
            rk = rk + jnp.where(b[r] > p2[j], 1.0, 0.0)
            lm = lm + jnp.where(p1[j] + b[r] >= thr, 1.0, 0.0)
        rank2.append(rk)
        lim1.append(jnp.where(p1[j] >= a[kk - 1], lm, 0.0))
    rank2 = jnp.concatenate(rank2, axis=0)
    lim1 = jnp.concatenate(lim1, axis=0)
    e2 = jnp.exp(s2 - b[0][0:1, :])
    g1 = jnp.exp(s1 - a[0][0:1, :]) / zsum[0:1, :]
    return flag, rank2, e2, lim1, g1


def _router_kernel(q_ref, k_ref, r2_ref, e2_ref, l1_ref, g1_ref):
    q = q_ref[...]
    half = q.shape[1] // 2
    s1 = _dot_nt(k_ref[0], q[:, 0:half].astype(BF16))
    s2 = _dot_nt(k_ref[1], q[:, half:].astype(BF16))
    flag, rank2, e2, lim1, g1 = _route_fast(s1, s2)
    r2_ref[...] = rank2.astype(BF16)
    l1_ref[...] = lim1
    e2_ref[...] = e2.astype(BF16)
    g1_ref[...] = g1

    @pl.when(flag > 0.0)
    def _():
        _route_exact(s1, s2, r2_ref, e2_ref, l1_ref, g1_ref)


def _route_exact(s1, s2, r2_ref, e2_ref, l1_ref, g1_ref):
    nk = PEER_NKEYS
    kk = PEER_TOPK
    t = s1.shape[1]
    krow = lax.broadcasted_iota(jnp.int32, (nk, t), 0)
    (a, ia), (b, ib) = _extract_top([s1, s2], [krow, krow], kk, nk)
    cand = jnp.concatenate([a[0:1, :] + b] + [a[r:r + 1, :] + b[0:8, :] for r in range(1, 8)]
                           + [a[8:16, :] + b[0:1, :]], axis=0)
    crow = lax.broadcasted_iota(jnp.int32, (80, t), 0)
    c_r1 = jnp.where(crow < 16, 0, jnp.where(crow < 72, jnp.right_shift(crow - 8, 3), crow - 64))
    c_r2 = jnp.where(crow < 16, crow, jnp.where(crow < 72, crow & 7, 0))
    flat = c_r1 * kk + c_r2
    ((cv, ci),) = _extract_top([cand], [flat], kk, kk * kk)
    zsum = jnp.sum(jnp.exp(cv - cv[0:1, :]), axis=0, keepdims=True)
    slot = lax.broadcasted_iota(jnp.int32, (kk, t), 0)
    r1 = jnp.right_shift(ci, 4)
    cnt = jnp.zeros((kk, t), F32)
    for r in range(kk):
        cnt = cnt + (slot == r1[r:r + 1, :]).astype(F32)
    rank2 = jnp.full((nk, t), float(kk), F32)
    lim1 = jnp.zeros((nk, t), F32)
    for r in range(kk):
        rank2 = jnp.where(krow == ib[r:r + 1, :], float(r), rank2)
        lim1 = jnp.where(krow == ia[r:r + 1, :], cnt[r:r + 1, :], lim1)
    r2_ref[...] = rank2.astype(BF16)
    l1_ref[...] = lim1
    e2_ref[...] = jnp.exp(s2 - b[0:1, :]).astype(BF16)
    g1_ref[...] = jnp.exp(s1 - a[0:1, :]) / zsum


def _router(qry, keys, tm=256):
    t, qw = qry.shape
    hq = qw // PEER_HEADS
    nk = PEER_NKEYS
    spec_o = pl.BlockSpec((None, nk, tm), lambda i, h: (h, 0, i))
    shp_b = jax.ShapeDtypeStruct((PEER_HEADS, nk, t), BF16)
    shp_f = jax.ShapeDtypeStruct((PEER_HEADS, nk, t), F32)
    return pl.pallas_call(
        _router_kernel,
        grid=(t // tm, PEER_HEADS),
        in_specs=[pl.BlockSpec((tm, hq), lambda i, h: (i, h)),
                  pl.BlockSpec((None, 2, nk, hq // 2), lambda i, h: (h, 0, 0, 0))],
        out_specs=[spec_o, spec_o, spec_o, spec_o],
        out_shape=[shp_b, shp_b, shp_f, shp_f],
        compiler_params=_params(("parallel", "arbitrary"), 32 << 20),
        name="router",
    )(qry, keys)


def _gelu(x):
    return 0.5 * x * (1.0 + lax.erf(x * (2.0 ** -0.5)))


def _ffn_kernel(ht_ref, u_ref, vt_ref, r2_ref, e2_ref, l1_ref, g1_ref, o_ref, acc_ref, *, te, tc, rc):
    e = pl.program_id(1)
    nk = PEER_NKEYS
    sub = BF16_SUBLANES
    d, tm = ht_ref.shape
    ns = te // nk
    nt = tm // tc

    @pl.when(e == 0)
    def _():
        acc_ref[...] = jnp.zeros_like(acc_ref)

    def scores(ti):
        return _dot(u_ref[...], ht_ref[:, ti * tc:(ti + 1) * tc])

    def gates(ti):
        c0 = ti * tc
        tiles = []
        for s in range(ns):
            lims = []
            gscs = []
            for hd in range(PEER_HEADS):
                lims.append(jnp.broadcast_to(l1_ref[hd, s:s + 1, c0:c0 + tc], (sub, tc)).astype(BF16))
                gscs.append(jnp.broadcast_to(g1_ref[hd, s:s + 1, c0:c0 + tc], (sub, tc)).astype(BF16))
            for j in range(nk // sub):
                w = jnp.zeros((sub, tc), BF16)
                for hd in range(PEER_HEADS):
                    r2 = r2_ref[hd, j * sub:(j + 1) * sub, c0:c0 + tc]
                    e2 = e2_ref[hd, j * sub:(j + 1) * sub, c0:c0 + tc]
                    w = w + jnp.where(r2 < lims[hd], e2, jnp.zeros_like(e2)) * gscs[hd]
                tiles.append(w)
        return jnp.concatenate(tiles, axis=0)

    def accumulate(ti, g):
        c0 = ti * tc
        for r0 in range(0, d, rc):
            acc_ref[r0:r0 + rc, c0:c0 + tc] += _dot(vt_ref[r0:r0 + rc, :], g)

    at = scores(0)
    w = gates(0)
    for ti in range(nt):
        at_next = scores(ti + 1) if ti + 1 < nt else None
        g = (_gelu(at) * w.astype(F32)).astype(BF16)
        w_next = gates(ti + 1) if ti + 1 < nt else None
        accumulate(ti, g)
        at, w = at_next, w_next

    @pl.when(e == pl.num_programs(1) - 1)
    def _():
        o_ref[...] = acc_ref[...].astype(o_ref.dtype)


def _ffn(ht, u, vt, r2, e2, l1, g1, tm=512, te=1024, tc=256, rc=1024):
    d, t = ht.shape
    ne = u.shape[0]
    nk = PEER_NKEYS
    ns = te // nk
    assert ns % 8 == 0
    once = pl.Buffered(1)
    rspec = pl.BlockSpec((PEER_HEADS, nk, tm), lambda i, e: (0, 0, i), pipeline_mode=once)
    kspec = pl.BlockSpec((PEER_HEADS, ns, tm), lambda i, e: (0, e, i))
    vmem = (d * tm * 2 + 2 * PEER_HEADS * nk * tm * 2 + 4 * PEER_HEADS * ns * tm * 4 + 2 * 2 * te * d * 2
            + d * tm * 4 + 2 * d * tm * 2 + 3 * te * tc * 4 + rc * tc * 4 + (2 << 20))
    return pl.pallas_call(
        functools.partial(_ffn_kernel, te=te, tc=tc, rc=rc),
        grid=(t // tm, ne // te),
        in_specs=[pl.BlockSpec((d, tm), lambda i, e: (0, i), pipeline_mode=once),
                  pl.BlockSpec((te, d), lambda i, e: (e, 0)),
                  pl.BlockSpec((d, te), lambda i, e: (0, e)),
                  rspec, rspec, kspec, kspec],
        out_specs=pl.BlockSpec((d, tm), lambda i, e: (0, i)),
        out_shape=jax.ShapeDtypeStruct((d, t), BF16),
        scratch_shapes=[pltpu.VMEM((d, tm), F32)],
        compiler_params=_params(("parallel", "arbitrary"), vmem),
        name="peer_ffn",
    )(ht, u, vt, r2, e2, l1, g1)


def _final_kernel(x_ref, ft_ref, g_ref, n_ref, o_ref):
    x = x_ref[...] + g_ref[...] * ft_ref[...].astype(F32).T
    ms = jnp.mean(x * x, axis=-1, keepdims=True)
    o_ref[...] = x * lax.rsqrt(ms + EPS) * n_ref[...]


def _final(x2d, ffn_t, gate, final_norm, tok_per_batch, tm=256):
    t, d = x2d.shape
    bpb = tok_per_batch // tm
    return pl.pallas_call(
        _final_kernel,
        grid=(t // tm,),
        in_specs=[pl.BlockSpec((tm, d), lambda i: (i, 0)),
                  pl.BlockSpec((d, tm), lambda i: (0, i)),
                  pl.BlockSpec((None, 1, d), lambda i: (i // bpb, 0, 0)),
                  pl.BlockSpec((1, d), lambda i: (0, 0))],
        out_specs=pl.BlockSpec((tm, d), lambda i: (i, 0)),
        out_shape=jax.ShapeDtypeStruct((t, d), F32),
        compiler_params=_params(("parallel",), 6 * tm * d * 4 + (8 << 20)),
        name="final",
    )(x2d, ffn_t, gate, final_norm.reshape(1, d))


def kernel(x, c, ctx, c_ctx, w_mod, b_mod, norm1, norm2, w_in, pool_w, pool_scale, conv_w, conv_b, dt_bias, a_log,
           d_skip, ssd_norm, w_out, peer_wq, peer_keys, peer_u, peer_v, final_norm):
    assert w_mod.shape[0] == 1, "single-layer trunk"
    bsz, n, d = x.shape
    n_ctx = ctx.shape[1]
    g = SSD_GROUPS
    inner = g * SSD_REP * SSD_HEAD_DIM
    gn = g * SSD_STATE
    pool_wd = POOL_GROUPS * pool_w.shape[2]
    off_z = pool_wd
    off_xbc = off_z + inner
    off_dt = off_xbc + inner + 2 * gn
    heads = g * SSD_REP

    cin = jnp.concatenate([c, c_ctx[None, :], jnp.zeros((8 - bsz - 1, d), F32)], axis=0)
    mod = _mod(cin, w_mod[0], b_mod[0])
    lat = mod[:bsz].reshape(bsz, 6, 1, d)
    sh1, sc1, g1, sh2, sc2, g2 = (lat[:, k] for k in range(6))
    cmod = mod[bsz].reshape(6, 1, 1, d)
    csh1, csc1 = cmod[0], cmod[1]

    w_in_b = w_in[0].astype(BF16)
    w_dt = jnp.pad(w_in_b[:, off_dt:], ((0, 0), (0, LANES - 2 * heads)))

    conv_wb = jnp.concatenate([conv_w[0], conv_b[0][None, :],
                               jnp.zeros((8 - SSD_CONV - 1, conv_w.shape[2]), F32)], axis=0)
    prm = jnp.stack([dt_bias[0].reshape(2, g, SSD_REP).transpose(1, 0, 2).reshape(g, 8),
                     a_log[0].reshape(2, g, SSD_REP).transpose(1, 0, 2).reshape(g, 8)], axis=-1)
    dsk = jnp.repeat(d_skip[0], SSD_HEAD_DIM).reshape(g, 1, SSD_REP * SSD_HEAD_DIM)

    def dt_rows(dt_out, nb, nt):
        r = dt_out[:, :2 * heads].reshape(nb, nt, 2, g, SSD_REP)
        return r.transpose(0, 3, 2, 4, 1).reshape(nb, g, 8, nt)

    pc, dtc = _nmm(ctx.reshape(bsz * n_ctx, d), norm1[0], csc1, csh1, w_in_b, col0=off_xbc, ncols=off_dt - off_xbc,
                   tok_per_batch=n_ctx, out_dtype=BF16, w_dt=w_dt, tm=256, tn=1024, name="inproj_ctx")
    zeros_h = jnp.zeros((bsz, g, SSD_STATE, 2 * SSD_REP * SSD_HEAD_DIM), F32)
    h_ctx = _ssd(pc.reshape(bsz, n_ctx, -1), 0, dt_rows(dtc, bsz, n_ctx), conv_wb, prm, dsk, zeros_h,
                 with_output=False)

    x2d = x.reshape(bsz * n, d)
    pm, dtl = _nmm(x2d, norm1[0], sc1, sh1, w_in_b, col0=0, ncols=off_dt, tok_per_batch=n, out_dtype=BF16,
                   w_dt=w_dt, tn=1024, name="inproj")
    pm3 = pm.reshape(bsz, n, off_dt)
    y = _ssd(pm3, off_xbc, dt_rows(dtl, bsz, n), conv_wb, prm, dsk, h_ctx, with_output=True)
    pooled = _pool(pm3, pool_w[0].astype(BF16), pool_scale[0].reshape(POOL_GROUPS, 1, -1))
    x1 = _oproj(pooled.reshape(bsz * n, pool_wd), y.reshape(bsz * n, inner), pm, off_z, ssd_norm[0],
                w_out[0].astype(BF16), x2d, g1, n)

    qry, h2t = _nmm(x1, norm2[0], sc2, sh2, peer_wq[0].astype(BF16), col0=0, ncols=peer_wq.shape[2],
                    tok_per_batch=n, out_dtype=F32, emit_ht=True, name="peer_q")
    r2, e2, l1, gg = _router(qry, peer_keys[0].astype(BF16))
    ffn_t = _ffn(h2t, peer_u[0].astype(BF16), peer_v[0].astype(BF16).T, r2, e2, l1, gg)
    out = _final(x1, ffn_t, g2, final_norm, n)
    return out.reshape(bsz, n, d)
```

```python
import functools

import jax
import jax.numpy as jnp
from jax import lax
from jax.experimental import pallas as pl
from jax.experimental.pallas import tpu as pltpu

F32 = jnp.float32
BF16 = jnp.bfloat16
HIGHEST = lax.Precision.HIGHEST

EPS = 1e-6
GRID_W = 64
POOL_GROUPS = 4
SSD_HEAD_DIM = 64
SSD_GROUPS = 8
SSD_REP = 4
SSD_STATE = 128
SSD_CONV = 5
SSD_CHUNK = 128
PEER_HEADS = 8
PEER_NKEYS = 128
PEER_TOPK = 16

LANES = 128
BF16_SUBLANES = 16
VMEM_CAP = 62 * 1024 * 1024


def _params(sem, vmem_bytes):
    return pltpu.CompilerParams(dimension_semantics=sem,
                                vmem_limit_bytes=int(min(max(vmem_bytes, 16 << 20), VMEM_CAP)))


def _dot(a, b):
    return jnp.dot(a, b, preferred_element_type=F32)


def _dot_nt(a, b):
    return lax.dot_general(a, b, (((1,), (1,)), ((), ())), preferred_element_type=F32)


def _silu(x):
    return x * jax.nn.sigmoid(x)


def _mod_kernel(c_ref, w_ref, b_ref, o_ref):
    s = _silu(c_ref[...]).astype(BF16)
    o_ref[...] = _dot(s, w_ref[...].astype(BF16)) + b_ref[...]


def _mod(cin, w, b):
    rows, d = cin.shape
    n = w.shape[1]
    tn = 512
    return pl.pallas_call(
        _mod_kernel,
        grid=(n // tn,),
        in_specs=[pl.BlockSpec((rows, d), lambda j: (0, 0)),
                  pl.BlockSpec((d, tn), lambda j: (0, j)),
                  pl.BlockSpec((1, tn), lambda j: (0, j))],
        out_specs=pl.BlockSpec((rows, tn), lambda j: (0, j)),
        out_shape=jax.ShapeDtypeStruct((rows, n), F32),
        compiler_params=_params(("arbitrary",), 2 * d * tn * 4 + (4 << 20)),
        name="mod",
    )(cin, w, b.reshape(1, n))


def _nmm_kernel(*refs, has_dt, emit_ht):
    x_ref, g_ref, sc_ref, sh_ref, w_ref = refs[:5]
    k = 5
    wdt_ref = None
    if has_dt:
        wdt_ref = refs[k]
        k += 1
    o_ref = refs[k]
    k += 1
    dt_ref = None
    if has_dt:
        dt_ref = refs[k]
        k += 1
    ht_ref = None
    if emit_ht:
        ht_ref = refs[k]
        k += 1
    h_ref = refs[k]

    @pl.when(pl.program_id(1) == 0)
    def _():
        x = x_ref[...]
        ms = jnp.mean(x * x, axis=-1, keepdims=True)
        y = x * lax.rsqrt(ms + EPS) * g_ref[...]
        h32 = y * (1.0 + sc_ref[...]) + sh_ref[...]
        h = h32.astype(BF16)
        h_ref[...] = h
        if has_dt:
            dt_ref[...] = _dot(h, wdt_ref[...])
        if emit_ht:
            for r0 in range(0, h32.shape[0], LANES):
                ht_ref[:, r0:r0 + LANES] = h32[r0:r0 + LANES, :].T.astype(BF16)

    o_ref[...] = _dot(h_ref[...], w_ref[...]).astype(o_ref.dtype)


def _nmm(x2d, gain, sc, sh, w, *, col0, ncols, tok_per_batch, out_dtype, w_dt=None, emit_ht=False,
         tm=512, tn=512, name="nmm"):
    t, d = x2d.shape
    nb = sc.shape[0]
    assert t % tm == 0 and ncols % tn == 0 and col0 % tn == 0 and tok_per_batch % tm == 0
    jb0 = col0 // tn
    if nb == 1:
        bmap = lambda i, j: (0, 0, 0)
    else:
        bpb = tok_per_batch // tm
        bmap = lambda i, j: (i // bpb, 0, 0)
    in_specs = [pl.BlockSpec((tm, d), lambda i, j: (i, 0)),
                pl.BlockSpec((1, d), lambda i, j: (0, 0)),
                pl.BlockSpec((None, 1, d), bmap),
                pl.BlockSpec((None, 1, d), bmap),
                pl.BlockSpec((d, tn), lambda i, j: (0, j + jb0))]
    args = [x2d, gain.reshape(1, d), sc, sh, w]
    out_specs = [pl.BlockSpec((tm, tn), lambda i, j: (i, j))]
    out_shape = [jax.ShapeDtypeStruct((t, ncols), out_dtype)]
    has_dt = w_dt is not None
    if has_dt:
        in_specs.append(pl.BlockSpec((d, LANES), lambda i, j: (0, 0)))
        args.append(w_dt)
        out_specs.append(pl.BlockSpec((tm, LANES), lambda i, j: (i, 0)))
        out_shape.append(jax.ShapeDtypeStruct((t, LANES), F32))
    if emit_ht:
        out_specs.append(pl.BlockSpec((d, tm), lambda i, j: (0, i)))
        out_shape.append(jax.ShapeDtypeStruct((d, t), BF16))
    vmem = (2 * tm * d * 4 + 2 * d * tn * 2 + tm * d * 2 + 2 * tm * tn * 4 + (2 * tm * d * 2 if emit_ht else 0)
            + (10 << 20))
    return pl.pallas_call(
        functools.partial(_nmm_kernel, has_dt=has_dt, emit_ht=emit_ht),
        grid=(t // tm, ncols // tn),
        in_specs=in_specs,
        out_specs=out_specs,
        out_shape=out_shape,
        scratch_shapes=[pltpu.VMEM((tm, d), BF16)],
        compiler_params=_params(("parallel", "arbitrary"), vmem),
        name=name,
    )(*args)


def _softplus(x):
    return jnp.maximum(x, 0.0) + jnp.log1p(jnp.exp(-jnp.abs(x)))


def _expand_heads(cols, first):
    rows = cols.shape[0]
    lane = lax.broadcasted_iota(jnp.int32, (rows, LANES), 1)
    parts = []
    for q in range(2):
        lo = cols[:, first + 2 * q:first + 2 * q + 1]
        hi = cols[:, first + 2 * q + 1:first + 2 * q + 2]
        parts.append(jnp.where(lane < SSD_HEAD_DIM, lo, hi))
    return jnp.concatenate(parts, axis=1)


def _ssd_kernel(xh_ref, bm_ref, cm_ref, cwx_ref, cwb_ref, cwc_ref, dtr_ref, prm_ref, dsk_ref, h0_ref,
                *rest, n_tok, with_output):
    if with_output:
        y_ref, xc_ref, ext_ref, st_ref, dec_ref, rows_ref, cols_ref = rest
        hout_ref = None
    else:
        hout_ref, xc_ref, ext_ref, st_ref, dec_ref, rows_ref, cols_ref = rest
        y_ref = None
    q = SSD_CHUNK
    nc = n_tok // q
    hw = SSD_REP * SSD_HEAD_DIM
    halo = 16

    def conv_body(c, carry):
        r0 = pl.multiple_of(c * q, q)
        rp = pl.multiple_of(jnp.maximum(c * q - halo, 0), halo)
        rn = pl.multiple_of(jnp.minimum(c * q + q, n_tok - halo), halo)
        keep_p = jnp.where(c > 0, 1.0, 0.0)
        keep_n = jnp.where(c < nc - 1, 1.0, 0.0)
        for ref, cw_ref, l0, wd in ((xh_ref, cwx_ref, 0, hw), (bm_ref, cwb_ref, hw, SSD_STATE),
                                    (cm_ref, cwc_ref, hw + SSD_STATE, SSD_STATE)):
            ext_ref[0:halo, l0:l0 + wd] = ref[pl.ds(rp, halo), :].astype(F32) * keep_p
            ext_ref[halo:halo + q, l0:l0 + wd] = ref[pl.ds(r0, q), :].astype(F32)
            ext_ref[halo + q:2 * halo + q, l0:l0 + wd] = ref[pl.ds(rn, halo), :].astype(F32) * keep_n
            cw = cw_ref[...]
            acc = jnp.zeros((q, wd), F32) + cw[SSD_CONV:SSD_CONV + 1, :]
            for k in range(SSD_CONV):
                off = halo - SSD_CONV // 2 + k
                acc = acc + cw[k:k + 1, :] * ext_ref[off:off + q, l0:l0 + wd]
            xc_ref[pl.ds(r0, q), l0:l0 + wd] = _silu(acc).astype(BF16)
        return carry

    lax.fori_loop(0, nc, conv_body, 0)

    prm = prm_ref[...]
    bias_r = prm[:, 0:1]
    a_r = -jnp.exp(prm[:, 1:2])
    dt_all = _softplus(dtr_ref[...] + bias_r)
    x_all = dt_all * a_r
    for c in range(nc):
        rows_ref[0, 8 * c:8 * c + 8, :] = x_all[:, c * q:(c + 1) * q]
        rows_ref[1, 8 * c:8 * c + 8, :] = dt_all[:, c * q:(c + 1) * q]
    xs = rows_ref[0]
    dts = rows_ref[1]
    jj = lax.broadcasted_iota(jnp.int32, (q, q), 0)
    ii = lax.broadcasted_iota(jnp.int32, (q, q), 1)
    tri_f = (jj <= ii).astype(F32)
    tri_b = (jj >= ii).astype(F32)
    cs_f = jnp.dot(xs, tri_f, precision=HIGHEST, preferred_element_type=F32)
    cs_b = jnp.dot(xs, tri_b, precision=HIGHEST, preferred_element_type=F32)
    rowk = lax.broadcasted_iota(jnp.int32, (8 * nc, q), 0) & 7
    cs = jnp.where(rowk < SSD_REP, cs_f, cs_b)
    total = jnp.sum(xs, axis=1, keepdims=True)
    wgt = jnp.exp(total - cs) * dts
    ecs = jnp.exp(cs)
    rows_ref[0] = cs
    lane_k = jnp.right_shift(lax.broadcasted_iota(jnp.int32, (8 * nc, 2 * hw), 1), 6)
    row_k = lax.broadcasted_iota(jnp.int32, (8 * nc, 2 * hw), 0) & 7
    dec = jnp.where(lane_k == row_k, jnp.exp(total), 0.0)
    pad = jnp.zeros((q - 24, q), F32)
    for c in range(nc):
        sl = slice(8 * c, 8 * c + 8)
        dec_ref[c] = dec[sl]
        cols_ref[c] = jnp.concatenate([cs[sl], wgt[sl], ecs[sl], pad], axis=0).T

    def state_body(c, carry):
        r0 = pl.multiple_of(c * q, q)
        cols = cols_ref[c]
        xc = xc_ref[pl.ds(r0, q), :]
        xh = xc[:, 0:hw].astype(F32)
        xw = jnp.concatenate([xh * _expand_heads(cols, 8), xh * _expand_heads(cols, 8 + SSD_REP)],
                             axis=1).astype(BF16)
        bm_t = xc[:, hw:hw + SSD_STATE].astype(F32).T.astype(BF16)
        st_ref[c] = _dot(bm_t, xw)
        return carry

    lax.fori_loop(0, nc, state_body, 0, unroll=2)

    def rec_body(i, carry):
        hf, hb = carry
        cf = i
        cb = nc - 1 - i
        sf = st_ref[cf, :, 0:hw]
        st_ref[cf, :, 0:hw] = hf
        hf = hf * jnp.sum(dec_ref[cf, :, 0:hw], axis=0, keepdims=True) + sf
        sb = st_ref[cb, :, hw:2 * hw]
        st_ref[cb, :, hw:2 * hw] = hb
        hb = hb * jnp.sum(dec_ref[cb, :, hw:2 * hw], axis=0, keepdims=True) + sb
        return hf, hb

    h0 = h0_ref[...]
    hf, hb = lax.fori_loop(0, nc, rec_body, (h0[:, 0:hw], h0[:, hw:2 * hw]))

    if not with_output:
        hout_ref[:, 0:hw] = hf
        hout_ref[:, hw:2 * hw] = hb
        return

    lane_head = jnp.right_shift(lax.broadcasted_iota(jnp.int32, (q, hw), 1), 6)
    dsk = dsk_ref[...]

    def out_body(c, carry):
        r0 = pl.multiple_of(c * q, q)
        r8 = pl.multiple_of(c * 8, 8)
        cols = cols_ref[c]
        cs_r = rows_ref[0, pl.ds(r8, 8), :]
        dt_r = rows_ref[1, pl.ds(r8, 8), :]
        xc = xc_ref[pl.ds(r0, q), :]
        xh_b = xc[:, 0:hw]
        bm = xc[:, hw:hw + SSD_STATE]
        cm = xc[:, hw + SSD_STATE:hw + 2 * SSD_STATE]
        cbm = _dot_nt(cm, bm)
        ydiag = jnp.zeros((q, hw), F32)
        for r in range(SSD_REP):
            seg_f = cols[:, r:r + 1] - cs_r[r:r + 1, :]
            l_f = jnp.exp(jnp.where(ii <= jj, seg_f, -jnp.inf)) * dt_r[r:r + 1, :]
            rb = SSD_REP + r
            seg_b = cols[:, rb:rb + 1] - cs_r[rb:rb + 1, :]
            l_b = jnp.exp(jnp.where(ii >= jj, seg_b, -jnp.inf)) * dt_r[rb:rb + 1, :]
            m = (cbm * (l_f + l_b)).astype(BF16)
            ydiag = ydiag + jnp.where(lane_head == r, _dot(m, xh_b), 0.0)
        st = st_ref[c]
        yo_f = _dot(cm, st[:, 0:hw].astype(BF16)) * _expand_heads(cols, 16)
        yo_b = _dot(cm, st[:, hw:2 * hw].astype(BF16)) * _expand_heads(cols, 16 + SSD_REP)
        y_ref[pl.ds(r0, q), :] = (ydiag + yo_f + yo_b + dsk * xh_b.astype(F32)).astype(y_ref.dtype)
        return carry

    lax.fori_loop(0, nc, out_body, 0, unroll=2)


def _ssd(xbc, col0, dtr, conv_wb, prm, dsk, h0, *, with_output):
    b, n, _ = xbc.shape
    g = SSD_GROUPS
    hw = SSD_REP * SSD_HEAD_DIM
    inner = g * hw
    assert col0 % hw == 0 and n % SSD_CHUNK == 0
    xb0 = col0 // hw
    bb0 = (col0 + inner) // SSD_STATE
    cb0 = (col0 + inner + g * SSD_STATE) // SSD_STATE
    nc = n // SSD_CHUNK
    in_specs = [
        pl.BlockSpec((None, n, hw), lambda bi, gi: (bi, 0, xb0 + gi)),
        pl.BlockSpec((None, n, SSD_STATE), lambda bi, gi: (bi, 0, bb0 + gi)),
        pl.BlockSpec((None, n, SSD_STATE), lambda bi, gi: (bi, 0, cb0 + gi)),
        pl.BlockSpec((8, hw), lambda bi, gi: (0, gi)),
        pl.BlockSpec((8, SSD_STATE), lambda bi, gi: (0, inner // SSD_STATE + gi)),
        pl.BlockSpec((8, SSD_STATE), lambda bi, gi: (0, (inner + g * SSD_STATE) // SSD_STATE + gi)),
        pl.BlockSpec((None, None, 8, n), lambda bi, gi: (bi, gi, 0, 0)),
        pl.BlockSpec((None, 8, 2), lambda bi, gi: (gi, 0, 0)),
        pl.BlockSpec((None, 1, hw), lambda bi, gi: (gi, 0, 0)),
        pl.BlockSpec((None, None, SSD_STATE, 2 * hw), lambda bi, gi: (bi, gi, 0, 0)),
    ]
    if with_output:
        out_specs = pl.BlockSpec((None, n, hw), lambda bi, gi: (bi, 0, gi))
        out_shape = jax.ShapeDtypeStruct((b, n, inner), BF16)
    else:
        out_specs = pl.BlockSpec((None, None, SSD_STATE, 2 * hw), lambda bi, gi: (bi, gi, 0, 0))
        out_shape = jax.ShapeDtypeStruct((b, g, SSD_STATE, 2 * hw), F32)
    vmem = (2 * n * 2 * hw * 2 + n * 2 * hw * 2 + nc * SSD_STATE * 2 * hw * 4 + 2 * n * hw * 4
            + 2 * 8 * n * 4 + nc * SSD_CHUNK * SSD_CHUNK * 4 + (8 << 20))
    return pl.pallas_call(
        functools.partial(_ssd_kernel, n_tok=n, with_output=with_output),
        grid=(b, g),
        in_specs=in_specs,
        out_specs=out_specs,
        out_shape=out_shape,
        scratch_shapes=[pltpu.VMEM((n, 2 * hw), BF16),
                        pltpu.VMEM((SSD_CHUNK + 32, 2 * hw), F32),
                        pltpu.VMEM((nc, SSD_STATE, 2 * hw), F32),
                        pltpu.VMEM((nc, 8, 2 * hw), F32),
                        pltpu.VMEM((2, 8 * nc, SSD_CHUNK), F32),
                        pltpu.VMEM((nc, SSD_CHUNK, SSD_CHUNK), F32)],
        compiler_params=_params(("parallel", "arbitrary"), vmem),
        name="ssd_out" if with_output else "ssd_state",
    )(xbc, xbc, xbc, conv_wb, conv_wb, conv_wb, dtr, prm, dsk, h0)


def _pool_kernel(v_ref, w_ref, sc_ref, o_ref, rm_ref, *, rows):
    k = pl.program_id(1)
    win = jnp.left_shift(2, k)
    half = win // 2
    gc = v_ref.shape[1]

    def row_body(r, carry):
        lo = jnp.maximum(r - half, 0)
        hi = jnp.minimum(r - half + win, rows)

        def acc_body(rr, acc):
            return acc + v_ref[pl.ds(pl.multiple_of(rr * GRID_W, GRID_W), GRID_W), :].astype(F32)

        tot = lax.fori_loop(lo, hi, acc_body, jnp.zeros((GRID_W, gc), F32))
        cnt = (hi - lo).astype(F32)
        rm_ref[pl.ds(pl.multiple_of(r * GRID_W, GRID_W), GRID_W), :] = (tot / cnt).astype(BF16)
        return carry

    lax.fori_loop(0, rows, row_body, 0)

    blk = min(8, rows) * GRID_W
    ti = lax.broadcasted_iota(jnp.int32, (blk, blk), 0)
    tj = lax.broadcasted_iota(jnp.int32, (blk, blk), 1)
    ci = ti & (GRID_W - 1)
    cj = tj & (GRID_W - 1)
    same_row = jnp.right_shift(ti, 6) == jnp.right_shift(tj, 6)
    band = jnp.where(same_row & (cj >= ci - half) & (cj < ci - half + win), 1.0, 0.0).astype(BF16)
    cc = lax.broadcasted_iota(jnp.int32, (blk, gc), 0) & (GRID_W - 1)
    cnt_c = (jnp.minimum(cc - half + win, GRID_W) - jnp.maximum(cc - half, 0)).astype(F32)
    w = w_ref[...]
    scale = sc_ref[...]

    def col_body(t, carry):
        r0 = pl.multiple_of(t * blk, blk)
        m = _dot(band, rm_ref[pl.ds(r0, blk), :]) / cnt_c
        p = (m - v_ref[pl.ds(r0, blk), :].astype(F32)).astype(BF16)
        o_ref[pl.ds(r0, blk), :] = (_dot(p, w) * scale).astype(o_ref.dtype)
        return carry

    lax.fori_loop(0, (rows * GRID_W) // blk, col_body, 0)


def _pool(pm, pool_w, pool_scale):
    b, n, _ = pm.shape
    gc = pool_w.shape[1]
    rows = n // GRID_W
    vmem = 2 * n * gc * 2 + n * gc * 2 + 2 * n * gc * 2 + 2 * gc * gc * 2 + (8 << 20)
    return pl.pallas_call(
        functools.partial(_pool_kernel, rows=rows),
        grid=(b, POOL_GROUPS),
        in_specs=[pl.BlockSpec((None, n, gc), lambda bi, ki: (bi, 0, ki)),
                  pl.BlockSpec((None, gc, gc), lambda bi, ki: (ki, 0, 0)),
                  pl.BlockSpec((None, 1, gc), lambda bi, ki: (ki, 0, 0))],
        out_specs=pl.BlockSpec((None, n, gc), lambda bi, ki: (bi, 0, ki)),
        out_shape=jax.ShapeDtypeStruct((b, n, POOL_GROUPS * gc), BF16),
        scratch_shapes=[pltpu.VMEM((n, gc), BF16)],
        compiler_params=_params(("parallel", "arbitrary"), vmem),
        name="pool",
    )(pm, pool_w, pool_scale)


def _oproj_kernel(pool_ref, y_ref, z_ref, nrm_ref, w_ref, x_ref, g_ref, o_ref, a_ref):
    pw = pool_ref.shape[1]

    @pl.when(pl.program_id(1) == 0)
    def _():
        a_ref[:, 0:pw] = pool_ref[...]
        rows = 256
        for r0 in range(0, a_ref.shape[0], rows):
            yz = y_ref[r0:r0 + rows, :].astype(F32) * _silu(z_ref[r0:r0 + rows, :].astype(F32))
            ms = jnp.mean(yz * yz, axis=-1, keepdims=True)
            a_ref[r0:r0 + rows, pw:] = (yz * lax.rsqrt(ms + EPS) * nrm_ref[...]).astype(BF16)

    o_ref[...] = x_ref[...] + g_ref[...] * _dot(a_ref[...], w_ref[...])


def _oproj(pooled, y, pm, z_col0, ssd_norm, w_out, x2d, gate, tok_per_batch, tm=1024, tn=512):
    t, d = x2d.shape
    pw = pooled.shape[1]
    sw = y.shape[1]
    assert z_col0 % sw == 0
    zb = z_col0 // sw
    bpb = tok_per_batch // tm
    vmem = 2 * tm * (pw * 2 + sw * 2 + sw * 2) + 2 * (pw + sw) * tn * 2 + tm * (pw + sw) * 2 + 4 * tm * tn * 4 + (12 << 20)
    return pl.pallas_call(
        _oproj_kernel,
        grid=(t // tm, d // tn),
        in_specs=[pl.BlockSpec((tm, pw), lambda i, j: (i, 0)),
                  pl.BlockSpec((tm, sw), lambda i, j: (i, 0)),
                  pl.BlockSpec((tm, sw), lambda i, j: (i, zb)),
                  pl.BlockSpec((1, sw), lambda i, j: (0, 0)),
                  pl.BlockSpec((pw + sw, tn), lambda i, j: (0, j)),
                  pl.BlockSpec((tm, tn), lambda i, j: (i, j)),
                  pl.BlockSpec((None, 1, tn), lambda i, j: (i // bpb, 0, j))],
        out_specs=pl.BlockSpec((tm, tn), lambda i, j: (i, j)),
        out_shape=jax.ShapeDtypeStruct((t, d), F32),
        scratch_shapes=[pltpu.VMEM((tm, pw + sw), BF16)],
        compiler_params=_params(("parallel", "arbitrary"), vmem),
        name="oproj",
    )(pooled, y, pm, ssd_norm.reshape(1, sw), w_out, x2d, gate)


def _extract_top(xs, ridxs, count, sentinel):
    t = xs[0].shape[1]
    slot = lax.broadcasted_iota(jnp.int32, (count, t), 0)

    def body(r, carry):
        out = []
        for (xx, vals, idxs), ridx in zip(carry, ridxs):
            m = jnp.max(xx, axis=0, keepdims=True)
            first = jnp.min(jnp.where(xx == m, ridx, sentinel), axis=0, keepdims=True)
            xx = jnp.where(ridx == first, -jnp.inf, xx)
            vals = jnp.where(slot == r, m, vals)
            idxs = jnp.where(slot == r, first, idxs)
            out.append((xx, vals, idxs))
        return tuple(out)

    init = tuple((x, jnp.zeros((count, t), F32), jnp.zeros((count, t), jnp.int32)) for x in xs)
    res = lax.fori_loop(0, count, body, init)
    return [(vals, idxs) for _, vals, idxs in res]


def _oddeven_pairs(n):
    pairs = []

    def merge(lo, m, r):
        step = r * 2
        if step < m:
            merge(lo, m, step)
            merge(lo + r, m, step)
            for i in range(lo + r, lo + m - r, step):
                pairs.append((i, i + r))
        else:
            pairs.append((lo, lo + r))

    def sort(lo, m):
        if m > 1:
            sort(lo, m // 2)
            sort(lo + m // 2, m // 2)
            merge(lo, m, 1)

    sort(0, n)
    return pairs


def _bitonic_pairs(n):
    pairs = []
    dist = n // 2
    while dist >= 1:
        pairs += [(i, i + dist) for i in range(n) if (i & dist) == 0]
        dist //= 2
    return pairs


_SORT_PAIRS = _oddeven_pairs(PEER_TOPK)
_MERGE_PAIRS = _bitonic_pairs(PEER_TOPK)


def _sorted_top(pieces):
    kk = PEER_TOPK
    v = list(pieces)

    def exchange(i, j):
        v[i], v[j] = jnp.maximum(v[i], v[j]), jnp.minimum(v[i], v[j])

    for i, j in _SORT_PAIRS:
        if j < len(v):
            exchange(i, j)
    v += [jnp.full(v[0].shape, -jnp.inf, F32)] * (kk - len(v))
    for shift in (4, 2, 1):
        other = [pltpu.roll(x, shift, 0) for x in v]
        v = [jnp.maximum(v[j], other[kk - 1 - j]) for j in range(kk)]
        for i, j in _MERGE_PAIRS:
            exchange(i, j)
    return v


def _route_fast(s1, s2):
    nk = PEER_NKEYS
    kk = PEER_TOPK
    t = s1.shape[1]
    p1 = [s1[8 * j:8 * j + 8, :] for j in range(nk // 8)]
    p2 = [s2[8 * j:8 * j + 8, :] for j in range(nk // 8)]
    a = _sorted_top(p1)
    b = _sorted_top(p2)
    sub = lax.broadcasted_iota(jnp.int32, (8, t), 0)

    def pack(vals):
        out = vals[7]
        for s in range(6, -1, -1):
            out = jnp.where(sub == s, vals[s], out)
        return out

    b_lo = pack(b[0:8])
    b_hi = pack(b[8:16])
    a_hi = pack(a[8:16])
    cands = [a[0] + b_lo, a[0] + b_hi] + [a[r] + b_lo for r in range(1, 8)] + [a_hi + b[0]]
    cv = _sorted_top(cands)
    thr = cv[kk - 1]
    zsum = jnp.zeros((8, t), F32)
    for r in range(kk):
        zsum = zsum + jnp.exp(cv[r] - cv[0])

    def count_ge(pieces, level):
        n = jnp.zeros((8, t), F32)
        for p in pieces:
            n = n + jnp.where(p >= level, 1.0, 0.0)
        return jnp.sum(n, axis=0, keepdims=True)

    tied = ((count_ge(cands, thr) != float(kk)) | (count_ge(p1, a[kk - 1]) != float(kk))
            | (count_ge(p2, b[kk - 1]) != float(kk)))
    dup = jnp.zeros((8, t), F32)
    for r in range(kk - 1):
        dup = jnp.maximum(dup, jnp.where((a[r] == a[r + 1]) | (b[r] == b[r + 1]), 1.0, 0.0))
    flag = jnp.max(jnp.maximum(jnp.where(tied, 1.0, 0.0), dup[0:1, :]))
    rank2 = []
    lim1 = []
    for j in range(nk // 8):
        rk = jnp.zeros((8, t), F32)
        lm = jnp.zeros((8, t), F32)
        for r in range(kk):
            rk = rk + jnp.where(b[r] > p2[j], 1.0, 0.0)
            lm = lm + jnp.where(p1[j] + b[r] >= thr, 1.0, 0.0)
        rank2.append(rk)
        lim1.append(jnp.where(p1[j] >= a[kk - 1], lm, 0.0))
    rank2 = jnp.concatenate(rank2, axis=0)
    lim1 = jnp.concatenate(lim1, axis=0)
    e2 = jnp.exp(s2 - b[0][0:1, :])
    g1 = jnp.exp(s1 - a[0][0:1, :]) / zsum[0:1, :]
    return flag, rank2, e2, lim1, g1


def _router_kernel(q_ref, k_ref, r2_ref, e2_ref, l1_ref, g1_ref):
    q = q_ref[...]
    half = q.shape[1] // 2
    s1 = _dot_nt(k_ref[0], q[:, 0:half].astype(BF16))
    s2 = _dot_nt(k_ref[1], q[:, half:].astype(BF16))
    flag, rank2, e2, lim1, g1 = _route_fast(s1, s2)
    r2_ref[...] = rank2.astype(BF16)
    l1_ref[...] = lim1
    e2_ref[...] = e2.astype(BF16)
    g1_ref[...] = g1

    @pl.when(flag > 0.0)
    def _():
        _route_exact(s1, s2, r2_ref, e2_ref, l1_ref, g1_ref)


def _route_exact(s1, s2, r2_ref, e2_ref, l1_ref, g1_ref):
    nk = PEER_NKEYS
    kk = PEER_TOPK
    t = s1.shape[1]
    krow = lax.broadcasted_iota(jnp.int32, (nk, t), 0)
    (a, ia), (b, ib) = _extract_top([s1, s2], [krow, krow], kk, nk)
    cand = jnp.concatenate([a[0:1, :] + b] + [a[r:r + 1, :] + b[0:8, :] for r in range(1, 8)]
                           + [a[8:16, :] + b[0:1, :]], axis=0)
    crow = lax.broadcasted_iota(jnp.int32, (80, t), 0)
    c_r1 = jnp.where(crow < 16, 0, jnp.where(crow < 72, jnp.right_shift(crow - 8, 3), crow - 64))
    c_r2 = jnp.where(crow < 16, crow, jnp.where(crow < 72, crow & 7, 0))
    flat = c_r1 * kk + c_r2
    ((cv, ci),) = _extract_top([cand], [flat], kk, kk * kk)
    zsum = jnp.sum(jnp.exp(cv - cv[0:1, :]), axis=0, keepdims=True)
    slot = lax.broadcasted_iota(jnp.int32, (kk, t), 0)
    r1 = jnp.right_shift(ci, 4)
    cnt = jnp.zeros((kk, t), F32)
    for r in range(kk):
        cnt = cnt + (slot == r1[r:r + 1, :]).astype(F32)
    rank2 = jnp.full((nk, t), float(kk), F32)
    lim1 = jnp.zeros((nk, t), F32)
    for r in range(kk):
        rank2 = jnp.where(krow == ib[r:r + 1, :], float(r), rank2)
        lim1 = jnp.where(krow == ia[r:r + 1, :], cnt[r:r + 1, :], lim1)
    r2_ref[...] = rank2.astype(BF16)
    l1_ref[...] = lim1
    e2_ref[...] = jnp.exp(s2 - b[0:1, :]).astype(BF16)
    g1_ref[...] = jnp.exp(s1 - a[0:1, :]) / zsum


def _router(qry, keys, tm=512):
    t, qw = qry.shape
    hq = qw // PEER_HEADS
    nk = PEER_NKEYS
    spec_o = pl.BlockSpec((None, nk, tm), lambda i, h: (h, 0, i))
    shp_b = jax.ShapeDtypeStruct((PEER_HEADS, nk, t), BF16)
    shp_f = jax.ShapeDtypeStruct((PEER_HEADS, nk, t), F32)
    return pl.pallas_call(
        _router_kernel,
        grid=(t // tm, PEER_HEADS),
        in_specs=[pl.BlockSpec((tm, hq), lambda i, h: (i, h)),
                  pl.BlockSpec((None, 2, nk, hq // 2), lambda i, h: (h, 0, 0, 0))],
        out_specs=[spec_o, spec_o, spec_o, spec_o],
        out_shape=[shp_b, shp_b, shp_f, shp_f],
        compiler_params=_params(("parallel", "arbitrary"), 32 << 20),
        name="router",
    )(qry, keys)


def _gelu(x):
    return 0.5 * x * (1.0 + lax.erf(x * (2.0 ** -0.5)))


def _ffn_kernel(ht_ref, u_ref, vt_ref, r2_ref, e2_ref, l1_ref, g1_ref, o_ref, acc_ref, *, te, tc, rc):
    e = pl.program_id(1)
    nk = PEER_NKEYS
    sub = BF16_SUBLANES
    d, tm = ht_ref.shape
    ns = te // nk
    nt = tm // tc

    @pl.when(e == 0)
    def _():
        acc_ref[...] = jnp.zeros_like(acc_ref)

    def scores(ti):
        return _dot(u_ref[...], ht_ref[:, ti * tc:(ti + 1) * tc])

    def gates(ti):
        c0 = ti * tc
        tiles = []
        for s in range(ns):
            lims = []
            gscs = []
            for hd in range(PEER_HEADS):
                lims.append(jnp.broadcast_to(l1_ref[hd, s:s + 1, c0:c0 + tc], (sub, tc)).astype(BF16))
                gscs.append(jnp.broadcast_to(g1_ref[hd, s:s + 1, c0:c0 + tc], (sub, tc)).astype(BF16))
            for j in range(nk // sub):
                w = jnp.zeros((sub, tc), BF16)
                for hd in range(PEER_HEADS):
                    r2 = r2_ref[hd, j * sub:(j + 1) * sub, c0:c0 + tc]
                    e2 = e2_ref[hd, j * sub:(j + 1) * sub, c0:c0 + tc]
                    w = w + jnp.where(r2 < lims[hd], e2, jnp.zeros_like(e2)) * gscs[hd]
                tiles.append(w)
        return jnp.concatenate(tiles, axis=0)

    def accumulate(ti, g):
        c0 = ti * tc
        for r0 in range(0, d, rc):
            acc_ref[r0:r0 + rc, c0:c0 + tc] += _dot(vt_ref[r0:r0 + rc, :], g)

    at = scores(0)
    w = gates(0)
    for ti in range(nt):
        at_next = scores(ti + 1) if ti + 1 < nt else None
        g = (_gelu(at) * w.astype(F32)).astype(BF16)
        w_next = gates(ti + 1) if ti + 1 < nt else None
        accumulate(ti, g)
        at, w = at_next, w_next

    @pl.when(e == pl.num_programs(1) - 1)
    def _():
        o_ref[...] = acc_ref[...].astype(o_ref.dtype)


def _ffn(ht, u, vt, r2, e2, l1, g1, tm=512, te=1024, tc=256, rc=1024):
    d, t = ht.shape
    ne = u.shape[0]
    nk = PEER_NKEYS
    ns = te // nk
    assert ns % 8 == 0
    once = pl.Buffered(1)
    rspec = pl.BlockSpec((PEER_HEADS, nk, tm), lambda i, e: (0, 0, i), pipeline_mode=once)
    kspec = pl.BlockSpec((PEER_HEADS, ns, tm), lambda i, e: (0, e, i))
    vmem = (d * tm * 2 + 2 * PEER_HEADS * nk * tm * 2 + 4 * PEER_HEADS * ns * tm * 4 + 2 * 2 * te * d * 2
            + d * tm * 4 + 2 * d * tm * 2 + 3 * te * tc * 4 + rc * tc * 4 + (2 << 20))
    return pl.pallas_call(
        functools.partial(_ffn_kernel, te=te, tc=tc, rc=rc),
        grid=(t // tm, ne // te),
        in_specs=[pl.BlockSpec((d, tm), lambda i, e: (0, i), pipeline_mode=once),
                  pl.BlockSpec((te, d), lambda i, e: (e, 0)),
                  pl.BlockSpec((d, te), lambda i, e: (0, e)),
                  rspec, rspec, kspec, kspec],
        out_specs=pl.BlockSpec((d, tm), lambda i, e: (0, i)),
        out_shape=jax.ShapeDtypeStruct((d, t), BF16),
        scratch_shapes=[pltpu.VMEM((d, tm), F32)],
        compiler_params=_params(("parallel", "arbitrary"), vmem),
        name="peer_ffn",
    )(ht, u, vt, r2, e2, l1, g1)


def _final_kernel(x_ref, ft_ref, g_ref, n_ref, o_ref):
    x = x_ref[...] + g_ref[...] * ft_ref[...].astype(F32).T
    ms = jnp.mean(x * x, axis=-1, keepdims=True)
    o_ref[...] = x * lax.rsqrt(ms + EPS) * n_ref[...]


def _final(x2d, ffn_t, gate, final_norm, tok_per_batch, tm=512):
    t, d = x2d.shape
    bpb = tok_per_batch // tm
    return pl.pallas_call(
        _final_kernel,
        grid=(t // tm,),
        in_specs=[pl.BlockSpec((tm, d), lambda i: (i, 0)),
                  pl.BlockSpec((d, tm), lambda i: (0, i)),
                  pl.BlockSpec((None, 1, d), lambda i: (i // bpb, 0, 0)),
                  pl.BlockSpec((1, d), lambda i: (0, 0))],
        out_specs=pl.BlockSpec((tm, d), lambda i: (i, 0)),
        out_shape=jax.ShapeDtypeStruct((t, d), F32),
        compiler_params=_params(("parallel",), 6 * tm * d * 4 + (8 << 20)),
        name="final",
    )(x2d, ffn_t, gate, final_norm.reshape(1, d))


def kernel(x, c, ctx, c_ctx, w_mod, b_mod, norm1, norm2, w_in, pool_w, pool_scale, conv_w, conv_b, dt_bias, a_log,
           d_skip, ssd_norm, w_out, peer_wq, peer_keys, peer_u, peer_v, final_norm):
    assert w_mod.shape[0] == 1, "single-layer trunk"
    bsz, n, d = x.shape
    n_ctx = ctx.shape[1]
    g = SSD_GROUPS
    inner = g * SSD_REP * SSD_HEAD_DIM
    gn = g * SSD_STATE
    pool_wd = POOL_GROUPS * pool_w.shape[2]
    off_z = pool_wd
    off_xbc = off_z + inner
    off_dt = off_xbc + inner + 2 * gn
    heads = g * SSD_REP

    cin = jnp.concatenate([c, c_ctx[None, :], jnp.zeros((8 - bsz - 1, d), F32)], axis=0)
    mod = _mod(cin, w_mod[0], b_mod[0])
    lat = mod[:bsz].reshape(bsz, 6, 1, d)
    sh1, sc1, g1, sh2, sc2, g2 = (lat[:, k] for k in range(6))
    cmod = mod[bsz].reshape(6, 1, 1, d)
    csh1, csc1 = cmod[0], cmod[1]

    w_in_b = w_in[0].astype(BF16)
    w_dt = jnp.pad(w_in_b[:, off_dt:], ((0, 0), (0, LANES - 2 * heads)))

    conv_wb = jnp.concatenate([conv_w[0], conv_b[0][None, :],
                               jnp.zeros((8 - SSD_CONV - 1, conv_w.shape[2]), F32)], axis=0)
    prm = jnp.stack([dt_bias[0].reshape(2, g, SSD_REP).transpose(1, 0, 2).reshape(g, 8),
                     a_log[0].reshape(2, g, SSD_REP).transpose(1, 0, 2).reshape(g, 8)], axis=-1)
    dsk = jnp.repeat(d_skip[0], SSD_HEAD_DIM).reshape(g, 1, SSD_REP * SSD_HEAD_DIM)

    def dt_rows(dt_out, nb, nt):
        r = dt_out[:, :2 * heads].reshape(nb, nt, 2, g, SSD_REP)
        return r.transpose(0, 3, 2, 4, 1).reshape(nb, g, 8, nt)

    pc, dtc = _nmm(ctx.reshape(bsz * n_ctx, d), norm1[0], csc1, csh1, w_in_b, col0=off_xbc, ncols=off_dt - off_xbc,
                   tok_per_batch=n_ctx, out_dtype=BF16, w_dt=w_dt, tm=256, tn=1024, name="inproj_ctx")
    zeros_h = jnp.zeros((bsz, g, SSD_STATE, 2 * SSD_REP * SSD_HEAD_DIM), F32)
    h_ctx = _ssd(pc.reshape(bsz, n_ctx, -1), 0, dt_rows(dtc, bsz, n_ctx), conv_wb, prm, dsk, zeros_h,
                 with_output=False)

    x2d = x.reshape(bsz * n, d)
    pm, dtl = _nmm(x2d, norm1[0], sc1, sh1, w_in_b, col0=0, ncols=off_dt, tok_per_batch=n, out_dtype=BF16,
                   w_dt=w_dt, tn=1024, name="inproj")
    pm3 = pm.reshape(bsz, n, off_dt)
    y = _ssd(pm3, off_xbc, dt_rows(dtl, bsz, n), conv_wb, prm, dsk, h_ctx, with_output=True)
    pooled = _pool(pm3, pool_w[0].astype(BF16), pool_scale[0].reshape(POOL_GROUPS, 1, -1))
    x1 = _oproj(pooled.reshape(bsz * n, pool_wd), y.reshape(bsz * n, inner), pm, off_z, ssd_norm[0],
                w_out[0].astype(BF16), x2d, g1, n)

    qry, h2t = _nmm(x1, norm2[0], sc2, sh2, peer_wq[0].astype(BF16), col0=0, ncols=peer_wq.shape[2],
                    tok_per_batch=n, out_dtype=F32, emit_ht=True, tn=1024, name="peer_q")
    r2, e2, l1, gg = _router(qry, peer_keys[0].astype(BF16))
    ffn_t = _ffn(h2t, peer_u[0].astype(BF16), peer_v[0].astype(BF16).T, r2, e2, l1, gg)
    out = _final(x1, ffn_t, g2, final_norm, n)
    return out.reshape(bsz, n, d)
```

```python
import functools

import jax
import jax.numpy as jnp
from jax import lax
from jax.experimental import pallas as pl
from jax.experimental.pallas import tpu as pltpu

F32 = jnp.float32
BF16 = jnp.bfloat16
HIGHEST = lax.Precision.HIGHEST

EPS = 1e-6
GRID_W = 64
POOL_GROUPS = 4
SSD_HEAD_DIM = 64
SSD_GROUPS = 8
SSD_REP = 4
SSD_STATE = 128
SSD_CONV = 5
SSD_CHUNK = 128
PEER_HEADS = 8
PEER_NKEYS = 128
PEER_TOPK = 16

LANES = 128
BF16_SUBLANES = 16
VMEM_CAP = 62 * 1024 * 1024


def _params(sem, vmem_bytes):
    return pltpu.CompilerParams(dimension_semantics=sem,
                                vmem_limit_bytes=int(min(max(vmem_bytes, 16 << 20), VMEM_CAP)))


def _dot(a, b):
    return jnp.dot(a, b, preferred_element_type=F32)


def _dot_nt(a, b):
    return lax.dot_general(a, b, (((1,), (1,)), ((), ())), preferred_element_type=F32)


def _silu(x):
    return x * jax.nn.sigmoid(x)


def _mod_kernel(c_ref, w_ref, b_ref, o_ref):
    s = _silu(c_ref[...]).astype(BF16)
    o_ref[...] = _dot(s, w_ref[...].astype(BF16)) + b_ref[...]


def _mod(cin, w, b):
    rows, d = cin.shape
    n = w.shape[1]
    tn = 512
    return pl.pallas_call(
        _mod_kernel,
        grid=(n // tn,),
        in_specs=[pl.BlockSpec((rows, d), lambda j: (0, 0)),
                  pl.BlockSpec((d, tn), lambda j: (0, j)),
                  pl.BlockSpec((1, tn), lambda j: (0, j))],
        out_specs=pl.BlockSpec((rows, tn), lambda j: (0, j)),
        out_shape=jax.ShapeDtypeStruct((rows, n), F32),
        compiler_params=_params(("arbitrary",), 2 * d * tn * 4 + (4 << 20)),
        name="mod",
    )(cin, w, b.reshape(1, n))


def _nmm_kernel(*refs, has_dt, emit_ht):
    x_ref, g_ref, sc_ref, sh_ref, w_ref = refs[:5]
    k = 5
    wdt_ref = None
    if has_dt:
        wdt_ref = refs[k]
        k += 1
    o_ref = refs[k]
    k += 1
    dt_ref = None
    if has_dt:
        dt_ref = refs[k]
        k += 1
    ht_ref = None
    if emit_ht:
        ht_ref = refs[k]
        k += 1
    h_ref = refs[k]

    @pl.when(pl.program_id(1) == 0)
    def _():
        x = x_ref[...]
        ms = jnp.mean(x * x, axis=-1, keepdims=True)
        y = x * lax.rsqrt(ms + EPS) * g_ref[...]
        h32 = y * (1.0 + sc_ref[...]) + sh_ref[...]
        h = h32.astype(BF16)
        h_ref[...] = h
        if has_dt:
            dt_ref[...] = _dot(h, wdt_ref[...])
        if emit_ht:
            for r0 in range(0, h32.shape[0], LANES):
                ht_ref[:, r0:r0 + LANES] = h32[r0:r0 + LANES, :].T.astype(BF16)

    o_ref[...] = _dot(h_ref[...], w_ref[...]).astype(o_ref.dtype)


def _nmm(x2d, gain, sc, sh, w, *, col0, ncols, tok_per_batch, out_dtype, w_dt=None, emit_ht=False,
         tm=512, tn=512, name="nmm"):
    t, d = x2d.shape
    nb = sc.shape[0]
    assert t % tm == 0 and ncols % tn == 0 and col0 % tn == 0 and tok_per_batch % tm == 0
    jb0 = col0 // tn
    if nb == 1:
        bmap = lambda i, j: (0, 0, 0)
    else:
        bpb = tok_per_batch // tm
        bmap = lambda i, j: (i // bpb, 0, 0)
    in_specs = [pl.BlockSpec((tm, d), lambda i, j: (i, 0)),
                pl.BlockSpec((1, d), lambda i, j: (0, 0)),
                pl.BlockSpec((None, 1, d), bmap),
                pl.BlockSpec((None, 1, d), bmap),
                pl.BlockSpec((d, tn), lambda i, j: (0, j + jb0))]
    args = [x2d, gain.reshape(1, d), sc, sh, w]
    out_specs = [pl.BlockSpec((tm, tn), lambda i, j: (i, j))]
    out_shape = [jax.ShapeDtypeStruct((t, ncols), out_dtype)]
    has_dt = w_dt is not None
    if has_dt:
        in_specs.append(pl.BlockSpec((d, LANES), lambda i, j: (0, 0)))
        args.append(w_dt)
        out_specs.append(pl.BlockSpec((tm, LANES), lambda i, j: (i, 0)))
        out_shape.append(jax.ShapeDtypeStruct((t, LANES), F32))
    if emit_ht:
        out_specs.append(pl.BlockSpec((d, tm), lambda i, j: (0, i)))
        out_shape.append(jax.ShapeDtypeStruct((d, t), BF16))
    vmem = (2 * tm * d * 4 + 2 * d * tn * 2 + tm * d * 2 + 2 * tm * tn * 4 + (2 * tm * d * 2 if emit_ht else 0)
            + (10 << 20))
    return pl.pallas_call(
        functools.partial(_nmm_kernel, has_dt=has_dt, emit_ht=emit_ht),
        grid=(t // tm, ncols // tn),
        in_specs=in_specs,
        out_specs=out_specs,
        out_shape=out_shape,
        scratch_shapes=[pltpu.VMEM((tm, d), BF16)],
        compiler_params=_params(("parallel", "arbitrary"), vmem),
        name=name,
    )(*args)


def _softplus(x):
    return jnp.maximum(x, 0.0) + jnp.log1p(jnp.exp(-jnp.abs(x)))


def _expand_heads(cols, first):
    rows = cols.shape[0]
    lane = lax.broadcasted_iota(jnp.int32, (rows, LANES), 1)
    parts = []
    for q in range(2):
        lo = cols[:, first + 2 * q:first + 2 * q + 1]
        hi = cols[:, first + 2 * q + 1:first + 2 * q + 2]
        parts.append(jnp.where(lane < SSD_HEAD_DIM, lo, hi))
    return jnp.concatenate(parts, axis=1)


def _ssd_kernel(xh_ref, bm_ref, cm_ref, cwx_ref, cwb_ref, cwc_ref, dtr_ref, prm_ref, dsk_ref, h0_ref,
                *rest, n_tok, with_output):
    if with_output:
        y_ref, xc_ref, ext_ref, st_ref, dec_ref, rows_ref, cols_ref = rest
        hout_ref = None
    else:
        hout_ref, xc_ref, ext_ref, st_ref, dec_ref, rows_ref, cols_ref = rest
        y_ref = None
    q = SSD_CHUNK
    nc = n_tok // q
    hw = SSD_REP * SSD_HEAD_DIM
    halo = 16

    def conv_body(c, carry):
        r0 = pl.multiple_of(c * q, q)
        rp = pl.multiple_of(jnp.maximum(c * q - halo, 0), halo)
        rn = pl.multiple_of(jnp.minimum(c * q + q, n_tok - halo), halo)
        keep_p = jnp.where(c > 0, 1.0, 0.0)
        keep_n = jnp.where(c < nc - 1, 1.0, 0.0)
        for ref, cw_ref, l0, wd in ((xh_ref, cwx_ref, 0, hw), (bm_ref, cwb_ref, hw, SSD_STATE),
                                    (cm_ref, cwc_ref, hw + SSD_STATE, SSD_STATE)):
            ext_ref[0:halo, l0:l0 + wd] = ref[pl.ds(rp, halo), :].astype(F32) * keep_p
            ext_ref[halo:halo + q, l0:l0 + wd] = ref[pl.ds(r0, q), :].astype(F32)
            ext_ref[halo + q:2 * halo + q, l0:l0 + wd] = ref[pl.ds(rn, halo), :].astype(F32) * keep_n
            cw = cw_ref[...]
            acc = jnp.zeros((q, wd), F32) + cw[SSD_CONV:SSD_CONV + 1, :]
            for k in range(SSD_CONV):
                off = halo - SSD_CONV // 2 + k
                acc = acc + cw[k:k + 1, :] * ext_ref[off:off + q, l0:l0 + wd]
            xc_ref[pl.ds(r0, q), l0:l0 + wd] = _silu(acc).astype(BF16)
        return carry

    lax.fori_loop(0, nc, conv_body, 0)

    prm = prm_ref[...]
    bias_r = prm[:, 0:1]
    a_r = -jnp.exp(prm[:, 1:2])
    dt_all = _softplus(dtr_ref[...] + bias_r)
    x_all = dt_all * a_r
    for c in range(nc):
        rows_ref[0, 8 * c:8 * c + 8, :] = x_all[:, c * q:(c + 1) * q]
        rows_ref[1, 8 * c:8 * c + 8, :] = dt_all[:, c * q:(c + 1) * q]
    xs = rows_ref[0]
    dts = rows_ref[1]
    jj = lax.broadcasted_iota(jnp.int32, (q, q), 0)
    ii = lax.broadcasted_iota(jnp.int32, (q, q), 1)
    tri_f = (jj <= ii).astype(F32)
    tri_b = (jj >= ii).astype(F32)
    cs_f = jnp.dot(xs, tri_f, precision=HIGHEST, preferred_element_type=F32)
    cs_b = jnp.dot(xs, tri_b, precision=HIGHEST, preferred_element_type=F32)
    rowk = lax.broadcasted_iota(jnp.int32, (8 * nc, q), 0) & 7
    cs = jnp.where(rowk < SSD_REP, cs_f, cs_b)
    total = jnp.sum(xs, axis=1, keepdims=True)
    wgt = jnp.exp(total - cs) * dts
    ecs = jnp.exp(cs)
    rows_ref[0] = cs
    lane_k = jnp.right_shift(lax.broadcasted_iota(jnp.int32, (8 * nc, 2 * hw), 1), 6)
    row_k = lax.broadcasted_iota(jnp.int32, (8 * nc, 2 * hw), 0) & 7
    dec = jnp.where(lane_k == row_k, jnp.exp(total), 0.0)
    pad = jnp.zeros((q - 24, q), F32)
    for c in range(nc):
        sl = slice(8 * c, 8 * c + 8)
        dec_ref[c] = dec[sl]
        cols_ref[c] = jnp.concatenate([cs[sl], wgt[sl], ecs[sl], pad], axis=0).T

    def state_body(c, carry):
        r0 = pl.multiple_of(c * q, q)
        cols = cols_ref[c]
        xc = xc_ref[pl.ds(r0, q), :]
        xh = xc[:, 0:hw].astype(F32)
        xw = jnp.concatenate([xh * _expand_heads(cols, 8), xh * _expand_heads(cols, 8 + SSD_REP)],
                             axis=1).astype(BF16)
        bm_t = xc[:, hw:hw + SSD_STATE].astype(F32).T.astype(BF16)
        st_ref[c] = _dot(bm_t, xw)
        return carry

    lax.fori_loop(0, nc, state_body, 0, unroll=2)

    def rec_body(i, carry):
        hf, hb = carry
        cf = i
        cb = nc - 1 - i
        sf = st_ref[cf, :, 0:hw]
        st_ref[cf, :, 0:hw] = hf
        hf = hf * jnp.sum(dec_ref[cf, :, 0:hw], axis=0, keepdims=True) + sf
        sb = st_ref[cb, :, hw:2 * hw]
        st_ref[cb, :, hw:2 * hw] = hb
        hb = hb * jnp.sum(dec_ref[cb, :, hw:2 * hw], axis=0, keepdims=True) + sb
        return hf, hb

    h0 = h0_ref[...]
    hf, hb = lax.fori_loop(0, nc, rec_body, (h0[:, 0:hw], h0[:, hw:2 * hw]))

    if not with_output:
        hout_ref[:, 0:hw] = hf
        hout_ref[:, hw:2 * hw] = hb
        return

    lane_head = jnp.right_shift(lax.broadcasted_iota(jnp.int32, (q, hw), 1), 6)
    dsk = dsk_ref[...]

    def out_body(c, carry):
        r0 = pl.multiple_of(c * q, q)
        r8 = pl.multiple_of(c * 8, 8)
        cols = cols_ref[c]
        cs_r = rows_ref[0, pl.ds(r8, 8), :]
        dt_r = rows_ref[1, pl.ds(r8, 8), :]
        xc = xc_ref[pl.ds(r0, q), :]
        xh_b = xc[:, 0:hw]
        bm = xc[:, hw:hw + SSD_STATE]
        cm = xc[:, hw + SSD_STATE:hw + 2 * SSD_STATE]
        cbm = _dot_nt(cm, bm)
        ydiag = jnp.zeros((q, hw), F32)
        for r in range(SSD_REP):
            seg_f = cols[:, r:r + 1] - cs_r[r:r + 1, :]
            l_f = jnp.exp(jnp.where(ii <= jj, seg_f, -jnp.inf)) * dt_r[r:r + 1, :]
            rb = SSD_REP + r
            seg_b = cols[:, rb:rb + 1] - cs_r[rb:rb + 1, :]
            l_b = jnp.exp(jnp.where(ii >= jj, seg_b, -jnp.inf)) * dt_r[rb:rb + 1, :]
            m = (cbm * (l_f + l_b)).astype(BF16)
            ydiag = ydiag + jnp.where(lane_head == r, _dot(m, xh_b), 0.0)
        st = st_ref[c]
        yo_f = _dot(cm, st[:, 0:hw].astype(BF16)) * _expand_heads(cols, 16)
        yo_b = _dot(cm, st[:, hw:2 * hw].astype(BF16)) * _expand_heads(cols, 16 + SSD_REP)
        y_ref[pl.ds(r0, q), :] = (ydiag + yo_f + yo_b + dsk * xh_b.astype(F32)).astype(y_ref.dtype)
        return carry

    lax.fori_loop(0, nc, out_body, 0, unroll=2)


def _ssd(xbc, col0, dtr, conv_wb, prm, dsk, h0, *, with_output):
    b, n, _ = xbc.shape
    g = SSD_GROUPS
    hw = SSD_REP * SSD_HEAD_DIM
    inner = g * hw
    assert col0 % hw == 0 and n % SSD_CHUNK == 0
    xb0 = col0 // hw
    bb0 = (col0 + inner) // SSD_STATE
    cb0 = (col0 + inner + g * SSD_STATE) // SSD_STATE
    nc = n // SSD_CHUNK
    in_specs = [
        pl.BlockSpec((None, n, hw), lambda bi, gi: (bi, 0, xb0 + gi)),
        pl.BlockSpec((None, n, SSD_STATE), lambda bi, gi: (bi, 0, bb0 + gi)),
        pl.BlockSpec((None, n, SSD_STATE), lambda bi, gi: (bi, 0, cb0 + gi)),
        pl.BlockSpec((8, hw), lambda bi, gi: (0, gi)),
        pl.BlockSpec((8, SSD_STATE), lambda bi, gi: (0, inner // SSD_STATE + gi)),
        pl.BlockSpec((8, SSD_STATE), lambda bi, gi: (0, (inner + g * SSD_STATE) // SSD_STATE + gi)),
        pl.BlockSpec((None, None, 8, n), lambda bi, gi: (bi, gi, 0, 0)),
        pl.BlockSpec((None, 8, 2), lambda bi, gi: (gi, 0, 0)),
        pl.BlockSpec((None, 1, hw), lambda bi, gi: (gi, 0, 0)),
        pl.BlockSpec((None, None, SSD_STATE, 2 * hw), lambda bi, gi: (bi, gi, 0, 0)),
    ]
    if with_output:
        out_specs = pl.BlockSpec((None, n, hw), lambda bi, gi: (bi, 0, gi))
        out_shape = jax.ShapeDtypeStruct((b, n, inner), BF16)
    else:
        out_specs = pl.BlockSpec((None, None, SSD_STATE, 2 * hw), lambda bi, gi: (bi, gi, 0, 0))
        out_shape = jax.ShapeDtypeStruct((b, g, SSD_STATE, 2 * hw), F32)
    vmem = (2 * n * 2 * hw * 2 + n * 2 * hw * 2 + nc * SSD_STATE * 2 * hw * 4 + 2 * n * hw * 4
            + 2 * 8 * n * 4 + nc * SSD_CHUNK * SSD_CHUNK * 4 + (8 << 20))
    return pl.pallas_call(
        functools.partial(_ssd_kernel, n_tok=n, with_output=with_output),
        grid=(b, g),
        in_specs=in_specs,
        out_specs=out_specs,
        out_shape=out_shape,
        scratch_shapes=[pltpu.VMEM((n, 2 * hw), BF16),
                        pltpu.VMEM((SSD_CHUNK + 32, 2 * hw), F32),
                        pltpu.VMEM((nc, SSD_STATE, 2 * hw), F32),
                        pltpu.VMEM((nc, 8, 2 * hw), F32),
                        pltpu.VMEM((2, 8 * nc, SSD_CHUNK), F32),
                        pltpu.VMEM((nc, SSD_CHUNK, SSD_CHUNK), F32)],
        compiler_params=_params(("parallel", "arbitrary"), vmem),
        name="ssd_out" if with_output else "ssd_state",
    )(xbc, xbc, xbc, conv_wb, conv_wb, conv_wb, dtr, prm, dsk, h0)


def _pool_kernel(v_ref, w_ref, sc_ref, o_ref, rm_ref, *, rows):
    k = pl.program_id(1)
    win = jnp.left_shift(2, k)
    half = win // 2
    gc = v_ref.shape[1]

    def tile(rr):
        return v_ref[pl.ds(pl.multiple_of(rr * GRID_W, GRID_W), GRID_W), :].astype(F32)

    def init_body(rr, acc):
        return acc + tile(rr)

    tot0 = lax.fori_loop(0, jnp.minimum(win - half, rows), init_body, jnp.zeros((GRID_W, gc), F32))

    def row_body(r, tot):
        lo = jnp.maximum(r - half, 0)
        hi = jnp.minimum(r - half + win, rows)
        cnt = (hi - lo).astype(F32)
        rm_ref[pl.ds(pl.multiple_of(r * GRID_W, GRID_W), GRID_W), :] = (tot / cnt).astype(BF16)
        add_r = jnp.minimum(r + 1 - half + win - 1, rows - 1)
        sub_r = jnp.maximum(r - half, 0)
        add_w = jnp.where(r + 1 - half + win <= rows, 1.0, 0.0)
        sub_w = jnp.where(r + 1 - half > 0, 1.0, 0.0)
        return tot + add_w * tile(add_r) - sub_w * tile(sub_r)

    lax.fori_loop(0, rows, row_body, tot0)

    blk = min(8, rows) * GRID_W
    ti = lax.broadcasted_iota(jnp.int32, (blk, blk), 0)
    tj = lax.broadcasted_iota(jnp.int32, (blk, blk), 1)
    ci = ti & (GRID_W - 1)
    cj = tj & (GRID_W - 1)
    same_row = jnp.right_shift(ti, 6) == jnp.right_shift(tj, 6)
    band = jnp.where(same_row & (cj >= ci - half) & (cj < ci - half + win), 1.0, 0.0).astype(BF16)
    cc = lax.broadcasted_iota(jnp.int32, (blk, gc), 0) & (GRID_W - 1)
    cnt_c = (jnp.minimum(cc - half + win, GRID_W) - jnp.maximum(cc - half, 0)).astype(F32)
    w = w_ref[...]
    scale = sc_ref[...]

    def col_body(t, carry):
        r0 = pl.multiple_of(t * blk, blk)
        m = _dot(band, rm_ref[pl.ds(r0, blk), :]) / cnt_c
        p = (m - v_ref[pl.ds(r0, blk), :].astype(F32)).astype(BF16)
        o_ref[pl.ds(r0, blk), :] = (_dot(p, w) * scale).astype(o_ref.dtype)
        return carry

    lax.fori_loop(0, (rows * GRID_W) // blk, col_body, 0)


def _pool(pm, pool_w, pool_scale):
    b, n, _ = pm.shape
    gc = pool_w.shape[1]
    rows = n // GRID_W
    vmem = 2 * n * gc * 2 + n * gc * 2 + 2 * n * gc * 2 + 2 * gc * gc * 2 + (8 << 20)
    return pl.pallas_call(
        functools.partial(_pool_kernel, rows=rows),
        grid=(b, POOL_GROUPS),
        in_specs=[pl.BlockSpec((None, n, gc), lambda bi, ki: (bi, 0, ki)),
                  pl.BlockSpec((None, gc, gc), lambda bi, ki: (ki, 0, 0)),
                  pl.BlockSpec((None, 1, gc), lambda bi, ki: (ki, 0, 0))],
        out_specs=pl.BlockSpec((None, n, gc), lambda bi, ki: (bi, 0, ki)),
        out_shape=jax.ShapeDtypeStruct((b, n, POOL_GROUPS * gc), BF16),
        scratch_shapes=[pltpu.VMEM((n, gc), BF16)],
        compiler_params=_params(("parallel", "arbitrary"), vmem),
        name="pool",
    )(pm, pool_w, pool_scale)


def _oproj_kernel(pool_ref, y_ref, z_ref, nrm_ref, w_ref, x_ref, g_ref, o_ref, a_ref):
    pw = pool_ref.shape[1]

    @pl.when(pl.program_id(1) == 0)
    def _():
        a_ref[:, 0:pw] = pool_ref[...]
        rows = 256
        for r0 in range(0, a_ref.shape[0], rows):
            yz = y_ref[r0:r0 + rows, :].astype(F32) * _silu(z_ref[r0:r0 + rows, :].astype(F32))
            ms = jnp.mean(yz * yz, axis=-1, keepdims=True)
            a_ref[r0:r0 + rows, pw:] = (yz * lax.rsqrt(ms + EPS) * nrm_ref[...]).astype(BF16)

    o_ref[...] = x_ref[...] + g_ref[...] * _dot(a_ref[...], w_ref[...])


def _oproj(pooled, y, pm, z_col0, ssd_norm, w_out, x2d, gate, tok_per_batch, tm=1024, tn=512):
    t, d = x2d.shape
    pw = pooled.shape[1]
    sw = y.shape[1]
    assert z_col0 % sw == 0
    zb = z_col0 // sw
    bpb = tok_per_batch // tm
    vmem = 2 * tm * (pw * 2 + sw * 2 + sw * 2) + 2 * (pw + sw) * tn * 2 + tm * (pw + sw) * 2 + 4 * tm * tn * 4 + (12 << 20)
    return pl.pallas_call(
        _oproj_kernel,
        grid=(t // tm, d // tn),
        in_specs=[pl.BlockSpec((tm, pw), lambda i, j: (i, 0)),
                  pl.BlockSpec((tm, sw), lambda i, j: (i, 0)),
                  pl.BlockSpec((tm, sw), lambda i, j: (i, zb)),
                  pl.BlockSpec((1, sw), lambda i, j: (0, 0)),
                  pl.BlockSpec((pw + sw, tn), lambda i, j: (0, j)),
                  pl.BlockSpec((tm, tn), lambda i, j: (i, j)),
                  pl.BlockSpec((None, 1, tn), lambda i, j: (i // bpb, 0, j))],
        out_specs=pl.BlockSpec((tm, tn), lambda i, j: (i, j)),
        out_shape=jax.ShapeDtypeStruct((t, d), F32),
        scratch_shapes=[pltpu.VMEM((tm, pw + sw), BF16)],
        compiler_params=_params(("parallel", "arbitrary"), vmem),
        name="oproj",
    )(pooled, y, pm, ssd_norm.reshape(1, sw), w_out, x2d, gate)


def _extract_top(xs, ridxs, count, sentinel):
    t = xs[0].shape[1]
    slot = lax.broadcasted_iota(jnp.int32, (count, t), 0)

    def body(r, carry):
        out = []
        for (xx, vals, idxs), ridx in zip(carry, ridxs):
            m = jnp.max(xx, axis=0, keepdims=True)
            first = jnp.min(jnp.where(xx == m, ridx, sentinel), axis=0, keepdims=True)
            xx = jnp.where(ridx == first, -jnp.inf, xx)
            vals = jnp.where(slot == r, m, vals)
            idxs = jnp.where(slot == r, first, idxs)
            out.append((xx, vals, idxs))
        return tuple(out)

    init = tuple((x, jnp.zeros((count, t), F32), jnp.zeros((count, t), jnp.int32)) for x in xs)
    res = lax.fori_loop(0, count, body, init)
    return [(vals, idxs) for _, vals, idxs in res]


def _oddeven_pairs(n):
    pairs = []

    def merge(lo, m, r):
        step = r * 2
        if step < m:
            merge(lo, m, step)
            merge(lo + r, m, step)
            for i in range(lo + r, lo + m - r, step):
                pairs.append((i, i + r))
        else:
            pairs.append((lo, lo + r))

    def sort(lo, m):
        if m > 1:
            sort(lo, m // 2)
            sort(lo + m // 2, m // 2)
            merge(lo, m, 1)

    sort(0, n)
    return pairs


def _bitonic_pairs(n):
    pairs = []
    dist = n // 2
    while dist >= 1:
        pairs += [(i, i + dist) for i in range(n) if (i & dist) == 0]
        dist //= 2
    return pairs


_SORT_PAIRS = _oddeven_pairs(PEER_TOPK)
_MERGE_PAIRS = _bitonic_pairs(PEER_TOPK)


def _sorted_top(pieces):
    kk = PEER_TOPK
    v = list(pieces)

    def exchange(i, j):
        v[i], v[j] = jnp.maximum(v[i], v[j]), jnp.minimum(v[i], v[j])

    for i, j in _SORT_PAIRS:
        if j < len(v):
            exchange(i, j)
    v += [jnp.full(v[0].shape, -jnp.inf, F32)] * (kk - len(v))
    for shift in (4, 2, 1):
        other = [pltpu.roll(x, shift, 0) for x in v]
        v = [jnp.maximum(v[j], other[kk - 1 - j]) for j in range(kk)]
        for i, j in _MERGE_PAIRS:
            exchange(i, j)
    return v


def _route_fast(s1, s2):
    nk = PEER_NKEYS
    kk = PEER_TOPK
    t = s1.shape[1]
    p1 = [s1[8 * j:8 * j + 8, :] for j in range(nk // 8)]
    p2 = [s2[8 * j:8 * j + 8, :] for j in range(nk // 8)]
    a = _sorted_top(p1)
    b = _sorted_top(p2)
    sub = lax.broadcasted_iota(jnp.int32, (8, t), 0)

    def pack(vals):
        out = vals[7]
        for s in range(6, -1, -1):
            out = jnp.where(sub == s, vals[s], out)
        return out

    b_lo = pack(b[0:8])
    b_hi = pack(b[8:16])
    a_hi = pack(a[8:16])
    cands = [a[0] + b_lo, a[0] + b_hi] + [a[r] + b_lo for r in range(1, 8)] + [a_hi + b[0]]
    cv = _sorted_top(cands)
    thr = cv[kk - 1]
    zsum = jnp.zeros((8, t), F32)
    for r in range(kk):
        zsum = zsum + jnp.exp(cv[r] - cv[0])

    def count_ge(pieces, level):
        n = jnp.zeros((8, t), F32)
        for p in pieces:
            n = n + jnp.where(p >= level, 1.0, 0.0)
        return jnp.sum(n, axis=0, keepdims=True)

    tied = ((count_ge(cands, thr) != float(kk)) | (count_ge(p1, a[kk - 1]) != float(kk))
            | (count_ge(p2, b[kk - 1]) != float(kk)))
    dup = jnp.zeros((8, t), F32)
    for r in range(kk - 1):
        dup = jnp.maximum(dup, jnp.where((a[r] == a[r + 1]) | (b[r] == b[r + 1]), 1.0, 0.0))
    flag = jnp.max(jnp.maximum(jnp.where(tied, 1.0, 0.0), dup[0:1, :]))
    rank2 = []
    lim1 = []
    for j in range(nk // 8):
        rk = jnp.zeros((8, t), F32)
        lm = jnp.zeros((8, t), F32)
        for r in range(kk):
            rk = rk + jnp.where(b[r] > p2[j], 1.0, 0.0)
            lm = lm + jnp.where(p1[j] + b[r] >= thr, 1.0, 0.0)
        rank2.append(rk)
        lim1.append(jnp.where(p1[j] >= a[kk - 1], lm, 0.0))
    rank2 = jnp.concatenate(rank2, axis=0)
    lim1 = jnp.concatenate(lim1, axis=0)
    e2 = jnp.exp(s2 - b[0][0:1, :])
    g1 = jnp.exp(s1 - a[0][0:1, :]) / zsum[0:1, :]
    return flag, rank2, e2, lim1, g1


def _router_kernel(q_ref, k_ref, r2_ref, e2_ref, l1_ref, g1_ref):
    q = q_ref[...]
    half = q.shape[1] // 2
    s1 = _dot_nt(k_ref[0], q[:, 0:half].astype(BF16))
    s2 = _dot_nt(k_ref[1], q[:, half:].astype(BF16))
    flag, rank2, e2, lim1, g1 = _route_fast(s1, s2)
    r2_ref[...] = rank2.astype(BF16)
    l1_ref[...] = lim1
    e2_ref[...] = e2.astype(BF16)
    g1_ref[...] = g1

    @pl.when(flag > 0.0)
    def _():
        _route_exact(s1, s2, r2_ref, e2_ref, l1_ref, g1_ref)


def _route_exact(s1, s2, r2_ref, e2_ref, l1_ref, g1_ref):
    nk = PEER_NKEYS
    kk = PEER_TOPK
    t = s1.shape[1]
    krow = lax.broadcasted_iota(jnp.int32, (nk, t), 0)
    (a, ia), (b, ib) = _extract_top([s1, s2], [krow, krow], kk, nk)
    cand = jnp.concatenate([a[0:1, :] + b] + [a[r:r + 1, :] + b[0:8, :] for r in range(1, 8)]
                           + [a[8:16, :] + b[0:1, :]], axis=0)
    crow = lax.broadcasted_iota(jnp.int32, (80, t), 0)
    c_r1 = jnp.where(crow < 16, 0, jnp.where(crow < 72, jnp.right_shift(crow - 8, 3), crow - 64))
    c_r2 = jnp.where(crow < 16, crow, jnp.where(crow < 72, crow & 7, 0))
    flat = c_r1 * kk + c_r2
    ((cv, ci),) = _extract_top([cand], [flat], kk, kk * kk)
    zsum = jnp.sum(jnp.exp(cv - cv[0:1, :]), axis=0, keepdims=True)
    slot = lax.broadcasted_iota(jnp.int32, (kk, t), 0)
    r1 = jnp.right_shift(ci, 4)
    cnt = jnp.zeros((kk, t), F32)
    for r in range(kk):
        cnt = cnt + (slot == r1[r:r + 1, :]).astype(F32)
    rank2 = jnp.full((nk, t), float(kk), F32)
    lim1 = jnp.zeros((nk, t), F32)
    for r in range(kk):
        rank2 = jnp.where(krow == ib[r:r + 1, :], float(r), rank2)
        lim1 = jnp.where(krow == ia[r:r + 1, :], cnt[r:r + 1, :], lim1)
    r2_ref[...] = rank2.astype(BF16)
    l1_ref[...] = lim1
    e2_ref[...] = jnp.exp(s2 - b[0:1, :]).astype(BF16)
    g1_ref[...] = jnp.exp(s1 - a[0:1, :]) / zsum


def _router(qry, keys, tm=512):
    t, qw = qry.shape
    hq = qw // PEER_HEADS
    nk = PEER_NKEYS
    spec_o = pl.BlockSpec((None, nk, tm), lambda i, h: (h, 0, i))
    shp_b = jax.ShapeDtypeStruct((PEER_HEADS, nk, t), BF16)
    shp_f = jax.ShapeDtypeStruct((PEER_HEADS, nk, t), F32)
    return pl.pallas_call(
        _router_kernel,
        grid=(t // tm, PEER_HEADS),
        in_specs=[pl.BlockSpec((tm, hq), lambda i, h: (i, h)),
                  pl.BlockSpec((None, 2, nk, hq // 2), lambda i, h: (h, 0, 0, 0))],
        out_specs=[spec_o, spec_o, spec_o, spec_o],
        out_shape=[shp_b, shp_b, shp_f, shp_f],
        compiler_params=_params(("parallel", "arbitrary"), 32 << 20),
        name="router",
    )(qry, keys)


def _gelu(x):
    return 0.5 * x * (1.0 + lax.erf(x * (2.0 ** -0.5)))


def _ffn_kernel(ht_ref, u_ref, vt_ref, r2_ref, e2_ref, l1_ref, g1_ref, o_ref, acc_ref, *, te, tc, rc):
    e = pl.program_id(1)
    nk = PEER_NKEYS
    sub = BF16_SUBLANES
    d, tm = ht_ref.shape
    ns = te // nk
    nt = tm // tc

    @pl.when(e == 0)
    def _():
        acc_ref[...] = jnp.zeros_like(acc_ref)

    def scores(ti):
        return _dot(u_ref[...], ht_ref[:, ti * tc:(ti + 1) * tc])

    def gates(ti):
        c0 = ti * tc
        tiles = []
        for s in range(ns):
            lims = []
            gscs = []
            for hd in range(PEER_HEADS):
                lims.append(jnp.broadcast_to(l1_ref[hd, s:s + 1, c0:c0 + tc], (sub, tc)).astype(BF16))
                gscs.append(jnp.broadcast_to(g1_ref[hd, s:s + 1, c0:c0 + tc], (sub, tc)).astype(BF16))
            for j in range(nk // sub):
                w = jnp.zeros((sub, tc), BF16)
                for hd in range(PEER_HEADS):
                    r2 = r2_ref[hd, j * sub:(j + 1) * sub, c0:c0 + tc]
                    e2 = e2_ref[hd, j * sub:(j + 1) * sub, c0:c0 + tc]
                    w = w + jnp.where(r2 < lims[hd], e2, jnp.zeros_like(e2)) * gscs[hd]
                tiles.append(w)
        return jnp.concatenate(tiles, axis=0)

    def accumulate(ti, g):
        c0 = ti * tc
        for r0 in range(0, d, rc):
            acc_ref[r0:r0 + rc, c0:c0 + tc] += _dot(vt_ref[r0:r0 + rc, :], g)

    at = scores(0)
    w = gates(0)
    for ti in range(nt):
        at_next = scores(ti + 1) if ti + 1 < nt else None
        g = (_gelu(at) * w.astype(F32)).astype(BF16)
        w_next = gates(ti + 1) if ti + 1 < nt else None
        accumulate(ti, g)
        at, w = at_next, w_next

    @pl.when(e == pl.num_programs(1) - 1)
    def _():
        o_ref[...] = acc_ref[...].astype(o_ref.dtype)


def _ffn(ht, u, vt, r2, e2, l1, g1, tm=512, te=1024, tc=256, rc=1024):
    d, t = ht.shape
    ne = u.shape[0]
    nk = PEER_NKEYS
    ns = te // nk
    assert ns % 8 == 0
    once = pl.Buffered(1)
    rspec = pl.BlockSpec((PEER_HEADS, nk, tm), lambda i, e: (0, 0, i), pipeline_mode=once)
    kspec = pl.BlockSpec((PEER_HEADS, ns, tm), lambda i, e: (0, e, i))
    vmem = (d * tm * 2 + 2 * PEER_HEADS * nk * tm * 2 + 4 * PEER_HEADS * ns * tm * 4 + 2 * 2 * te * d * 2
            + d * tm * 4 + 2 * d * tm * 2 + 3 * te * tc * 4 + rc * tc * 4 + (2 << 20))
    return pl.pallas_call(
        functools.partial(_ffn_kernel, te=te, tc=tc, rc=rc),
        grid=(t // tm, ne // te),
        in_specs=[pl.BlockSpec((d, tm), lambda i, e: (0, i), pipeline_mode=once),
                  pl.BlockSpec((te, d), lambda i, e: (e, 0)),
                  pl.BlockSpec((d, te), lambda i, e: (0, e)),
                  rspec, rspec, kspec, kspec],
        out_specs=pl.BlockSpec((d, tm), lambda i, e: (0, i)),
        out_shape=jax.ShapeDtypeStruct((d, t), BF16),
        scratch_shapes=[pltpu.VMEM((d, tm), F32)],
        compiler_params=_params(("parallel", "arbitrary"), vmem),
        name="peer_ffn",
    )(ht, u, vt, r2, e2, l1, g1)


def _final_kernel(x_ref, ft_ref, g_ref, n_ref, o_ref):
    x = x_ref[...] + g_ref[...] * ft_ref[...].astype(F32).T
    ms = jnp.mean(x * x, axis=-1, keepdims=True)
    o_ref[...] = x * lax.rsqrt(ms + EPS) * n_ref[...]


def _final(x2d, ffn_t, gate, final_norm, tok_per_batch, tm=512):
    t, d = x2d.shape
    bpb = tok_per_batch // tm
    return pl.pallas_call(
        _final_kernel,
        grid=(t // tm,),
        in_specs=[pl.BlockSpec((tm, d), lambda i: (i, 0)),
                  pl.BlockSpec((d, tm), lambda i: (0, i)),
                  pl.BlockSpec((None, 1, d), lambda i: (i // bpb, 0, 0)),
                  pl.BlockSpec((1, d), lambda i: (0, 0))],
        out_specs=pl.BlockSpec((tm, d), lambda i: (i, 0)),
        out_shape=jax.ShapeDtypeStruct((t, d), F32),
        compiler_params=_params(("parallel",), 6 * tm * d * 4 + (8 << 20)),
        name="final",
    )(x2d, ffn_t, gate, final_norm.reshape(1, d))


def kernel(x, c, ctx, c_ctx, w_mod, b_mod, norm1, norm2, w_in, pool_w, pool_scale, conv_w, conv_b, dt_bias, a_log,
           d_skip, ssd_norm, w_out, peer_wq, peer_keys, peer_u, peer_v, final_norm):
    assert w_mod.shape[0] == 1, "single-layer trunk"
    bsz, n, d = x.shape
    n_ctx = ctx.shape[1]
    g = SSD_GROUPS
    inner = g * SSD_REP * SSD_HEAD_DIM
    gn = g * SSD_STATE
    pool_wd = POOL_GROUPS * pool_w.shape[2]
    off_z = pool_wd
    off_xbc = off_z + inner
    off_dt = off_xbc + inner + 2 * gn
    heads = g * SSD_REP

    cin = jnp.concatenate([c, c_ctx[None, :], jnp.zeros((8 - bsz - 1, d), F32)], axis=0)
    mod = _mod(cin, w_mod[0], b_mod[0])
    lat = mod[:bsz].reshape(bsz, 6, 1, d)
    sh1, sc1, g1, sh2, sc2, g2 = (lat[:, k] for k in range(6))
    cmod = mod[bsz].reshape(6, 1, 1, d)
    csh1, csc1 = cmod[0], cmod[1]

    w_in_b = w_in[0].astype(BF16)
    w_dt = jnp.pad(w_in_b[:, off_dt:], ((0, 0), (0, LANES - 2 * heads)))

    conv_wb = jnp.concatenate([conv_w[0], conv_b[0][None, :],
                               jnp.zeros((8 - SSD_CONV - 1, conv_w.shape[2]), F32)], axis=0)
    prm = jnp.stack([dt_bias[0].reshape(2, g, SSD_REP).transpose(1, 0, 2).reshape(g, 8),
                     a_log[0].reshape(2, g, SSD_REP).transpose(1, 0, 2).reshape(g, 8)], axis=-1)
    dsk = jnp.repeat(d_skip[0], SSD_HEAD_DIM).reshape(g, 1, SSD_REP * SSD_HEAD_DIM)

    def dt_rows(dt_out, nb, nt):
        r = dt_out[:, :2 * heads].reshape(nb, nt, 2, g, SSD_REP)
        return r.transpose(0, 3, 2, 4, 1).reshape(nb, g, 8, nt)

    pc, dtc = _nmm(ctx.reshape(bsz * n_ctx, d), norm1[0], csc1, csh1, w_in_b, col0=off_xbc, ncols=off_dt - off_xbc,
                   tok_per_batch=n_ctx, out_dtype=BF16, w_dt=w_dt, tm=256, tn=1024, name="inproj_ctx")
    zeros_h = jnp.zeros((bsz, g, SSD_STATE, 2 * SSD_REP * SSD_HEAD_DIM), F32)
    h_ctx = _ssd(pc.reshape(bsz, n_ctx, -1), 0, dt_rows(dtc, bsz, n_ctx), conv_wb, prm, dsk, zeros_h,
                 with_output=False)

    x2d = x.reshape(bsz * n, d)
    pm, dtl = _nmm(x2d, norm1[0], sc1, sh1, w_in_b, col0=0, ncols=off_dt, tok_per_batch=n, out_dtype=BF16,
                   w_dt=w_dt, tn=1024, name="inproj")
    pm3 = pm.reshape(bsz, n, off_dt)
    y = _ssd(pm3, off_xbc, dt_rows(dtl, bsz, n), conv_wb, prm, dsk, h_ctx, with_output=True)
    pooled = _pool(pm3, pool_w[0].astype(BF16), pool_scale[0].reshape(POOL_GROUPS, 1, -1))
    x1 = _oproj(pooled.reshape(bsz * n, pool_wd), y.reshape(bsz * n, inner), pm, off_z, ssd_norm[0],
                w_out[0].astype(BF16), x2d, g1, n)

    qry, h2t = _nmm(x1, norm2[0], sc2, sh2, peer_wq[0].astype(BF16), col0=0, ncols=peer_wq.shape[2],
                    tok_per_batch=n, out_dtype=F32, emit_ht=True, tn=1024, name="peer_q")
    r2, e2, l1, gg = _router(qry, peer_keys[0].astype(BF16))
    ffn_t = _ffn(h2t, peer_u[0].astype(BF16), peer_v[0].astype(BF16).T, r2, e2, l1, gg)
    out = _final(x1, ffn_t, g2, final_norm, n)
    return out.reshape(bsz, n, d)
```
